```python
import jax, jax.numpy as jnp
from jax import lax
import numpy as np

D_MODEL = 1024
BATCH = 4
SEQ = 8192
DEPTH = 2
DEC_BATCH = 128
DEC_SEQ = 8
PAST_LEN = 16384
PAGE_SIZE = 128

N_A = DEPTH // 2
N_B = DEPTH - N_A
D_CONV = D_MODEL
CONV_WIDTH = 31
CONV_STATE = CONV_WIDTH - 1
N_HEADS = 16
N_KV_HEADS = 4
HEAD_DIM = 64
GROUP = N_HEADS // N_KV_HEADS
D_ATTN = N_HEADS * HEAD_DIM
D_KV = N_KV_HEADS * HEAD_DIM
WINDOW = 128
Q_BLOCK = 128
RMS_EPS = 1e-6
LN_EPS = 1e-5

kernel_name = "yoco_conformer_conv_swa_sink_decoder_step"


def rms_norm(x, g):
    xf = x.astype(jnp.float32)
    y = xf * lax.rsqrt(jnp.mean(xf * xf, axis=-1, keepdims=True) + RMS_EPS)
    return (y * g.astype(jnp.float32)).astype(x.dtype)


def layer_norm(x, g, b):
    xf = x.astype(jnp.float32)
    mu = jnp.mean(xf, axis=-1, keepdims=True)
    xc = xf - mu
    var = jnp.mean(xc * xc, axis=-1, keepdims=True)
    return (xc * lax.rsqrt(var + LN_EPS) * g.astype(jnp.float32) + b.astype(jnp.float32)).astype(x.dtype)


def alibi_slopes():
    s = 2.0 ** (-8.0 * np.arange(1, N_HEADS + 1) / N_HEADS)
    return jnp.asarray(s, jnp.float32).reshape(N_KV_HEADS, GROUP)


def conv_mixer(x, prefix, g_norm, w_in, w_dw, b_dw, ln_g, ln_b, w_out):
    h = rms_norm(x, g_norm)
    u = jnp.einsum('btd,de->bte', h, w_in)
    a, gl, z = jnp.split(u, 3, axis=-1)
    v = a * jax.nn.sigmoid(gl)
    if prefix is None:
        prefix = jnp.zeros((x.shape[0], CONV_STATE, D_CONV), v.dtype)
    vfull = jnp.concatenate([prefix.astype(v.dtype), v], axis=1)
    c = lax.conv_general_dilated(
        vfull, w_dw[:, None, :].astype(v.dtype), window_strides=(1,), padding='VALID',
        dimension_numbers=('NWC', 'WIO', 'NWC'), feature_group_count=D_CONV)
    c = c + b_dw
    c = jax.nn.silu(layer_norm(c, ln_g, ln_b)) * jax.nn.silu(z)
    out = jnp.einsum('btc,cd->btd', c, w_out)
    return x + out, vfull[:, -CONV_STATE:]


def shared_kv(x, g_kv, w_kv, k_norm_g):
    b, t, _ = x.shape
    h = rms_norm(x, g_kv)
    kv = jnp.einsum('btd,de->bte', h, w_kv)
    k, v = jnp.split(kv, 2, axis=-1)
    k = rms_norm(k.reshape(b, t, N_KV_HEADS, HEAD_DIM), k_norm_g)
    return k, v.reshape(b, t, N_KV_HEADS, HEAD_DIM)


def query_side(x, g_norm, w_in, q_norm_g):
    b, t, _ = x.shape
    h = rms_norm(x, g_norm)
    u = jnp.einsum('btd,de->bte', h, w_in)
    q, z = jnp.split(u, 2, axis=-1)
    q = rms_norm(q.reshape(b, t, N_HEADS, HEAD_DIM), q_norm_g)
    return q, z


def band_attention(q, k_ctx, v_ctx, key_valid, sinks):
    b, tq = q.shape[:2]
    tk = k_ctx.shape[1]
    qg = q.reshape(b, tq, N_KV_HEADS, GROUP, HEAD_DIM)
    s = jnp.einsum('bqkgd,bskd->bkgqs', qg, k_ctx,
                   preferred_element_type=jnp.float32) * (HEAD_DIM ** -0.5)
    dist = jnp.arange(tq)[:, None] + WINDOW - jnp.arange(tk)[None, :]
    allowed = (dist >= 0) & (dist < WINDOW) & key_valid[None, :]
    s = s - alibi_slopes()[..., None, None] * dist.astype(jnp.float32)
    s = jnp.where(allowed, s, -jnp.inf)
    sink = sinks.astype(jnp.float32).reshape(N_KV_HEADS, GROUP)[..., None, None]
    m = jnp.maximum(jnp.max(s, axis=-1, keepdims=True), sink)
    p = jnp.exp(s - m)
    denom = jnp.sum(p, axis=-1, keepdims=True) + jnp.exp(sink - m)
    p = (p / denom).astype(v_ctx.dtype)
    o = jnp.einsum('bkgqs,bskd->bqkgd', p, v_ctx)
    return o.reshape(b, tq, D_ATTN)


def prompt_attention(q, k, v, sinks):
    b, t = q.shape[:2]
    nb = t // Q_BLOCK
    span = WINDOW + Q_BLOCK
    pad = ((0, 0), (WINDOW, 0), (0, 0), (0, 0))
    kp = jnp.pad(k, pad)
    vp = jnp.pad(v, pad)

    def block(i):
        start = i * Q_BLOCK
        qb = lax.dynamic_slice_in_dim(q, start, Q_BLOCK, axis=1)
        kb = lax.dynamic_slice_in_dim(kp, start, span, axis=1)
        vb = lax.dynamic_slice_in_dim(vp, start, span, axis=1)
        valid = (start - WINDOW + jnp.arange(span)) >= 0
        return band_attention(qb, kb, vb, valid, sinks)

    o = lax.map(block, jnp.arange(nb))
    return jnp.moveaxis(o, 0, 1).reshape(b, t, D_ATTN)


def attn_output(x, o, z, w_out):
    return x + jnp.einsum('bte,ed->btd', o * jax.nn.silu(z), w_out)


def setup_inputs(seed: int = 0) -> dict:
    key = jax.random.key(seed)
    ks = jax.random.split(key, 24)
    f32 = jnp.float32

    def nrm(k, shape, scale):
        return jax.random.normal(k, shape, f32) * scale

    def gain(k, shape):
        return 1.0 + 0.01 * jax.random.normal(k, shape, f32)

    return {
        "x_prompt": nrm(ks[0], (BATCH, SEQ, D_MODEL), 1.0),
        "x_sample": nrm(ks[1], (DEC_BATCH, DEC_SEQ, D_MODEL), 1.0),
        "state_conv": nrm(ks[2], (N_A, DEC_BATCH, CONV_STATE, D_CONV), 0.5),
        "cache_k": nrm(ks[3], (DEC_BATCH, WINDOW, N_KV_HEADS, HEAD_DIM), 1.0),
        "cache_v": nrm(ks[4], (DEC_BATCH, WINDOW, N_KV_HEADS, HEAD_DIM), 1.0),
        "norm_a": gain(ks[5], (N_A, D_MODEL)),
        "w_in_a": nrm(ks[6], (N_A, D_MODEL, 3 * D_CONV), D_MODEL ** -0.5),
        "w_dw_a": nrm(ks[7], (N_A, CONV_WIDTH, D_CONV), CONV_WIDTH ** -0.5),
        "b_dw_a": nrm(ks[8], (N_A, D_CONV), 0.01),
        "ln_g_a": gain(ks[9], (N_A, D_CONV)),
        "ln_b_a": nrm(ks[10], (N_A, D_CONV), 0.01),
        "w_out_a": nrm(ks[11], (N_A, D_CONV, D_MODEL), D_CONV ** -0.5),
        "norm_kv": gain(ks[12], (D_MODEL,)),
        "w_kv": nrm(ks[13], (D_MODEL, 2 * D_KV), D_MODEL ** -0.5),
        "k_norm": gain(ks[14], (HEAD_DIM,)),
        "norm_b": gain(ks[15], (N_B, D_MODEL)),
        "w_in_b": nrm(ks[16], (N_B, D_MODEL, 2 * D_ATTN), D_MODEL ** -0.5),
        "q_norm": gain(ks[17], (N_B, HEAD_DIM)),
        "sinks_b": nrm(ks[18], (N_B, N_HEADS), 1.0),
        "w_out_b": nrm(ks[19], (N_B, D_ATTN, D_MODEL), D_ATTN ** -0.5),
    }


def reference(x_prompt, x_sample, state_conv, cache_k, cache_v,
              norm_a, w_in_a, w_dw_a, b_dw_a, ln_g_a, ln_b_a, w_out_a,
              norm_kv, w_kv, k_norm,
              norm_b, w_in_b, q_norm, sinks_b, w_out_b):
    xp, xs = x_prompt, x_sample
    conv_p, conv_s = [], []
    kp = vp = kctx_s = vctx_s = None
    for layer in range(DEPTH):
        if layer < N_A:
            i = layer
            args = (norm_a[i], w_in_a[i], w_dw_a[i], b_dw_a[i], ln_g_a[i], ln_b_a[i], w_out_a[i])
            xp, st_p = conv_mixer(xp, None, *args)
            xs, st_s = conv_mixer(xs, state_conv[i], *args)
            conv_p.append(st_p)
            conv_s.append(st_s)
        else:
            j = layer - N_A
            if j == 0:
                kp, vp = shared_kv(xp, norm_kv, w_kv, k_norm)
                kn, vn = shared_kv(xs, norm_kv, w_kv, k_norm)
                kctx_s = jnp.concatenate([cache_k.astype(kn.dtype), kn], axis=1)
                vctx_s = jnp.concatenate([cache_v.astype(vn.dtype), vn], axis=1)
            qp, zp = query_side(xp, norm_b[j], w_in_b[j], q_norm[j])
            op = prompt_attention(qp, kp, vp, sinks_b[j])
            xp = attn_output(xp, op, zp, w_out_b[j])
            qs, zs = query_side(xs, norm_b[j], w_in_b[j], q_norm[j])
            valid_s = jnp.ones((kctx_s.shape[1],), bool)
            os_ = band_attention(qs, kctx_s, vctx_s, valid_s, sinks_b[j])
            xs = attn_output(xs, os_, zs, w_out_b[j])
    new_conv_prompt = jnp.stack(conv_p, axis=0)
    new_conv_sample = jnp.stack(conv_s, axis=0)
    new_k_prompt = kp[:, -WINDOW:]
    new_v_prompt = vp[:, -WINDOW:]
    new_k_sample = kctx_s[:, -WINDOW:]
    new_v_sample = vctx_s[:, -WINDOW:]
    return (xp, xs, new_conv_prompt, new_conv_sample, new_k_prompt, new_v_prompt, new_k_sample, new_v_sample)
```

```python
import functools

import numpy as np
import jax
import jax.numpy as jnp
from jax import lax
from jax.experimental import pallas as pl
from jax.experimental.pallas import tpu as pltpu

F32 = jnp.float32
BF16 = jnp.bfloat16

D_MODEL = 1024
D_CONV = 1024
CONV_WIDTH = 31
CONV_STATE = CONV_WIDTH - 1
N_HEADS = 16
N_KV = 4
GROUP = N_HEADS // N_KV
HEAD_DIM = 64
D_ATTN = N_HEADS * HEAD_DIM
D_KV = N_KV * HEAD_DIM
WINDOW = 128
Q_BLOCK = 128
RMS_EPS = 1e-6
LN_EPS = 1e-5

SUBLANES = 8
LANES = 128
LANE_CHUNKS = D_CONV // LANES
assert LANE_CHUNKS == SUBLANES
BF16_ROWS = 16
PREFIX_ROWS = 32
PREFIX_SKIP = PREFIX_ROWS - CONV_STATE
TILE_M = 512
CONV_ROWS = 32
SAMPLE_BLOCK = 32
SAMPLE_BLOCK_ATTN = 16
VMEM_LIMIT = 56 * 1024 * 1024


def _dot(a, b):
    return jnp.dot(a, b, preferred_element_type=F32)


def _dot_nt(a, b):
    return lax.dot_general(a, b, (((1,), (1,)), ((), ())), preferred_element_type=F32)


def _rms_unit(x):
    return x * lax.rsqrt(jnp.mean(x * x, axis=-1, keepdims=True) + RMS_EPS)


def _silu(x):
    return x * jax.nn.sigmoid(x)


def _head_rms(x, seg_mean, gain):
    ms = _dot((x * x).astype(BF16), seg_mean)
    return x * lax.rsqrt(ms + RMS_EPS) * gain


def _conv_ln_gate(tap, wdw_ref, bdw, lng, lnb, zgate):
    acc = tap(0) * wdw_ref[0:1, :] + bdw
    for k in range(1, CONV_WIDTH):
        acc = acc + tap(k) * wdw_ref[k:k + 1, :]
    mu = jnp.mean(acc, axis=-1, keepdims=True)
    xc = acc - mu
    var = jnp.mean(xc * xc, axis=-1, keepdims=True)
    y = xc * lax.rsqrt(var + LN_EPS) * lng + lnb
    return _silu(y) * zgate


def _mixer_a_prompt_kernel(x_ref, g_ref, win_ref, wdw_ref, bdw_ref, lng_ref, lnb_ref, wout_ref,
                           y_ref, st_ref, hist_ref, tmp_ref, zg_ref, act_ref):
    t = pl.program_id(1)
    tm = x_ref.shape[1]
    prefix = PREFIX_ROWS * LANE_CHUNKS

    @pl.when(t == 0)
    def _():
        hist_ref[0:prefix, :] = jnp.zeros((prefix, LANES), F32)

    x = x_ref[0]
    h = (_rms_unit(x) * g_ref[...]).astype(BF16)
    a = _dot(h, win_ref[:, 0:D_CONV])
    gl = _dot(h, win_ref[:, D_CONV:2 * D_CONV])
    v = a * jax.nn.sigmoid(gl)
    for c in range(LANE_CHUNKS):
        hist_ref[pl.ds(prefix + c, tm, stride=LANE_CHUNKS), :] = v[:, c * LANES:(c + 1) * LANES]
    zg_ref[...] = _silu(_dot(h, win_ref[:, 2 * D_CONV:3 * D_CONV]))

    @pl.when(t == pl.num_programs(1) - 1)
    def _():
        st_ref[0] = v[tm - PREFIX_ROWS:tm, :]

    lng = lng_ref[...]
    lnb = lnb_ref[...]
    rows = CONV_ROWS * LANE_CHUNKS

    def chunk(i, carry):
        r0 = pl.multiple_of(i * CONV_ROWS, CONV_ROWS)
        base = pl.multiple_of((r0 + PREFIX_SKIP) * LANE_CHUNKS, LANE_CHUNKS)
        acc = None
        for k in range(CONV_WIDTH):
            tap = hist_ref[pl.ds(base + k * LANE_CHUNKS, rows), :].reshape(CONV_ROWS, LANE_CHUNKS, LANES)
            term = tap * wdw_ref[k * LANE_CHUNKS:(k + 1) * LANE_CHUNKS, :]
            acc = term + bdw_ref[...] if acc is None else acc + term
        tmp_ref[...] = acc.reshape(rows, LANES)
        conv = jnp.concatenate(
            [tmp_ref[pl.ds(c, CONV_ROWS, stride=LANE_CHUNKS), :] for c in range(LANE_CHUNKS)], axis=1)
        mu = jnp.mean(conv, axis=-1, keepdims=True)
        xc = conv - mu
        var = jnp.mean(xc * xc, axis=-1, keepdims=True)
        y = xc * lax.rsqrt(var + LN_EPS) * lng + lnb
        act_ref[pl.ds(r0, CONV_ROWS), :] = (_silu(y) * zg_ref[pl.ds(r0, CONV_ROWS), :]).astype(BF16)
        return carry

    lax.fori_loop(0, tm // CONV_ROWS, chunk, 0)

    y_ref[0] = x + _dot(act_ref[...], wout_ref[...])

    hist_ref[0:prefix, :] = hist_ref[tm * LANE_CHUNKS:tm * LANE_CHUNKS + prefix, :]


def _mixer_a_prompt(x, g, w_in, w_dw, b_dw, ln_g, ln_b, w_out):
    b, t, d = x.shape
    tm = TILE_M
    const = lambda shape: pl.BlockSpec(shape, lambda i, j: (0,) * len(shape))
    return pl.pallas_call(
        _mixer_a_prompt_kernel,
        grid=(b, t // tm),
        in_specs=[
            pl.BlockSpec((1, tm, d), lambda i, j: (i, j, 0)),
            const((1, d)),
            const((d, 3 * D_CONV)),
            const((CONV_WIDTH * LANE_CHUNKS, LANES)),
            const((LANE_CHUNKS, LANES)),
            const((1, D_CONV)),
            const((1, D_CONV)),
            const((D_CONV, d)),
        ],
        out_specs=[
            pl.BlockSpec((1, tm, d), lambda i, j: (i, j, 0)),
            pl.BlockSpec((1, PREFIX_ROWS, D_CONV), lambda i, j: (i, 0, 0)),
        ],
        out_shape=[
            jax.ShapeDtypeStruct((b, t, d), F32),
            jax.ShapeDtypeStruct((b, PREFIX_ROWS, D_CONV), F32),
        ],
        scratch_shapes=[
            pltpu.VMEM(((PREFIX_ROWS + tm) * LANE_CHUNKS, LANES), F32),
            pltpu.VMEM((CONV_ROWS * LANE_CHUNKS, LANES), F32),
            pltpu.VMEM((tm, D_CONV), F32),
            pltpu.VMEM((tm, D_CONV), BF16),
        ],
        compiler_params=pltpu.CompilerParams(
            dimension_semantics=("arbitrary", "arbitrary"), vmem_limit_bytes=VMEM_LIMIT),
        name="mixer_a_prompt",
    )(x, g, w_in, w_dw.reshape(CONV_WIDTH * LANE_CHUNKS, LANES), b_dw.reshape(LANE_CHUNKS, LANES),
      ln_g, ln_b, w_out)


def _mixer_a_sample_kernel(x_ref, st_ref, g_ref, win_ref, wdw_ref, bdw_ref, lng_ref, lnb_ref, wout_ref,
                           y_ref, nst_ref, vf_ref, zg_ref, act_ref):
    nb, ts, d = x_ref.shape
    x = x_ref[...].reshape(nb * ts, d)
    h = (_rms_unit(x) * g_ref[...]).astype(BF16)
    a = _dot(h, win_ref[:, 0:D_CONV])
    gl = _dot(h, win_ref[:, D_CONV:2 * D_CONV])
    v = a * jax.nn.sigmoid(gl)
    zg_ref[...] = _silu(_dot(h, win_ref[:, 2 * D_CONV:3 * D_CONV])).reshape(nb, ts, D_CONV)
    vf_ref[:, 0:CONV_STATE, :] = st_ref[0]
    vf_ref[:, CONV_STATE:CONV_STATE + ts, :] = v.reshape(nb, ts, D_CONV)
    nst_ref[0] = vf_ref[:, ts:ts + CONV_STATE, :]

    bdw = bdw_ref[...]
    lng = lng_ref[...]
    lnb = lnb_ref[...]
    cb = CONV_ROWS // ts

    def chunk(i, carry):
        b0 = pl.multiple_of(i * cb, cb)

        def tap(k):
            return vf_ref[pl.ds(b0, cb), k:k + ts, :].reshape(cb * ts, D_CONV)

        zgate = zg_ref[pl.ds(b0, cb), :, :].reshape(cb * ts, D_CONV)
        gated = _conv_ln_gate(tap, wdw_ref, bdw, lng, lnb, zgate)
        act_ref[pl.ds(b0, cb), :, :] = gated.astype(BF16).reshape(cb, ts, D_CONV)
        return carry

    lax.fori_loop(0, nb // cb, chunk, 0)

    out = _dot(act_ref[...].reshape(nb * ts, D_CONV), wout_ref[...])
    y_ref[...] = (x + out).reshape(nb, ts, d)


def _mixer_a_sample(x, state, g, w_in, w_dw, b_dw, ln_g, ln_b, w_out):
    b, ts, d = x.shape
    nb = SAMPLE_BLOCK
    const = lambda shape: pl.BlockSpec(shape, lambda i: (0,) * len(shape))
    return pl.pallas_call(
        _mixer_a_sample_kernel,
        grid=(b // nb,),
        in_specs=[
            pl.BlockSpec((nb, ts, d), lambda i: (i, 0, 0)),
            pl.BlockSpec((1, nb, CONV_STATE, D_CONV), lambda i: (0, i, 0, 0)),
            const((1, d)),
            const((d, 3 * D_CONV)),
            const((CONV_WIDTH, D_CONV)),
            const((1, D_CONV)),
            const((1, D_CONV)),
            const((1, D_CONV)),
            const((D_CONV, d)),
        ],
        out_specs=[
            pl.BlockSpec((nb, ts, d), lambda i: (i, 0, 0)),
            pl.BlockSpec((1, nb, CONV_STATE, D_CONV), lambda i: (0, i, 0, 0)),
        ],
        out_shape=[
            jax.ShapeDtypeStruct((b, ts, d), F32),
            jax.ShapeDtypeStruct((1, b, CONV_STATE, D_CONV), F32),
        ],
        scratch_shapes=[
            pltpu.VMEM((nb, CONV_STATE + ts + 2, D_CONV), F32),
            pltpu.VMEM((nb, ts, D_CONV), F32),
            pltpu.VMEM((nb, ts, D_CONV), BF16),
        ],
        compiler_params=pltpu.CompilerParams(
            dimension_semantics=("arbitrary",), vmem_limit_bytes=VMEM_LIMIT),
        name="mixer_a_sample",
    )(x, state, g, w_in, w_dw, b_dw, ln_g, ln_b, w_out)


def _project_kvqz(x, gkv_ref, gb_ref, wkv_ref, wq_ref, wz_ref, kg_ref, qg_ref, seg_ref):
    xn = _rms_unit(x)
    hk = (xn * gkv_ref[...]).astype(BF16)
    hq = (xn * gb_ref[...]).astype(BF16)
    seg = seg_ref[...]
    k = _head_rms(_dot(hk, wkv_ref[:, 0:D_KV]), seg, kg_ref[...])
    v = _dot(hk, wkv_ref[:, D_KV:2 * D_KV])
    q = []
    for g in range(GROUP):
        cols = slice(g * D_KV, (g + 1) * D_KV)
        q.append(_head_rms(_dot(hq, wq_ref[:, cols]), seg, qg_ref[...]))
    zg = _silu(_dot(hq, wz_ref[...]))
    return k, v, q, zg


def _lane_segment_mask(kvh):
    lane = lax.broadcasted_iota(jnp.int32, (1, D_KV), 1)
    return (lane >= kvh * HEAD_DIM) & (lane < (kvh + 1) * HEAD_DIM)


def _softmax_unnorm(s, sink):
    m = jnp.maximum(jnp.max(s, axis=-1, keepdims=True), sink)
    p = jnp.exp(s - m)
    denom = jnp.sum(p, axis=-1, keepdims=True) + jnp.exp(sink - m)
    return p, denom


def _mixer_b_prompt_kernel(sink_ref, x_ref, gkv_ref, gb_ref, wkv_ref, wq_ref, wz_ref, kg_ref, qg_ref,
                           seg_ref, bias_ref, wout_ref,
                           y_ref, kt_ref, vt_ref,
                           km_ref, vm_ref, q_ref, o_ref):
    t = pl.program_id(1)
    tm = x_ref.shape[1]

    @pl.when(t == 0)
    def _():
        km_ref[:, 0:WINDOW, :] = jnp.zeros((N_KV, WINDOW, D_KV), BF16)
        vm_ref[:, 0:WINDOW, :] = jnp.zeros((N_KV, WINDOW, D_KV), BF16)

    x = x_ref[0]
    k, v, q, zg = _project_kvqz(x, gkv_ref, gb_ref, wkv_ref, wq_ref, wz_ref, kg_ref, qg_ref, seg_ref)
    for g in range(GROUP):
        q_ref[g] = q[g].astype(BF16)
    for kvh in range(N_KV):
        mask = _lane_segment_mask(kvh)
        km_ref[kvh, WINDOW:WINDOW + tm, :] = jnp.where(mask, k, 0.0).astype(BF16)
        vm_ref[kvh, WINDOW:WINDOW + tm, :] = jnp.where(mask, v, 0.0).astype(BF16)

    @pl.when(t == pl.num_programs(1) - 1)
    def _():
        kt_ref[0] = k[tm - WINDOW:tm, :]
        vt_ref[0] = v[tm - WINDOW:tm, :]

    span = WINDOW + Q_BLOCK

    def block(j, carry):
        r0 = pl.multiple_of(j * Q_BLOCK, Q_BLOCK)
        first = jnp.logical_and(t == 0, j == 0)
        bsel = jnp.where(first, 0, 1)
        qs = [q_ref[g, pl.ds(r0, Q_BLOCK), :] for g in range(GROUP)]
        outs = [None] * GROUP
        for kvh in range(N_KV):
            kk = km_ref[kvh, pl.ds(r0, span), :]
            vv = vm_ref[kvh, pl.ds(r0, span), :]
            for g in range(GROUP):
                head = kvh * GROUP + g
                s = _dot_nt(qs[g], kk) + bias_ref[bsel, head]
                p, denom = _softmax_unnorm(s, sink_ref[head])
                o = _dot(p.astype(BF16), vv) / denom
                outs[g] = o if outs[g] is None else outs[g] + o
        for g in range(GROUP):
            o_ref[pl.ds(r0, Q_BLOCK), g * D_KV:(g + 1) * D_KV] = outs[g]
        return carry

    lax.fori_loop(0, tm // Q_BLOCK, block, 0)

    y_ref[0] = x + _dot((o_ref[...] * zg).astype(BF16), wout_ref[...])

    km_ref[:, 0:WINDOW, :] = km_ref[:, tm:tm + WINDOW, :]
    vm_ref[:, 0:WINDOW, :] = vm_ref[:, tm:tm + WINDOW, :]


def _mixer_b_prompt(x, sinks, gkv, gb, wkv, wq, wz, kg, qg, seg, bias, wout):
    b, t, d = x.shape
    tm = TILE_M
    const = lambda shape: pl.BlockSpec(shape, lambda i, j: (0,) * len(shape))
    return pl.pallas_call(
        _mixer_b_prompt_kernel,
        grid=(b, t // tm),
        in_specs=[
            pl.BlockSpec(memory_space=pltpu.SMEM),
            pl.BlockSpec((1, tm, d), lambda i, j: (i, j, 0)),
            const((1, d)),
            const((1, d)),
            const((d, 2 * D_KV)),
            const((d, D_ATTN)),
            const((d, D_ATTN)),
            const((1, D_KV)),
            const((1, D_KV)),
            const((D_KV, D_KV)),
            const((2, N_HEADS, Q_BLOCK, WINDOW + Q_BLOCK)),
            const((D_ATTN, d)),
        ],
        out_specs=[
            pl.BlockSpec((1, tm, d), lambda i, j: (i, j, 0)),
            pl.BlockSpec((1, WINDOW, D_KV), lambda i, j: (i, 0, 0)),
            pl.BlockSpec((1, WINDOW, D_KV), lambda i, j: (i, 0, 0)),
        ],
        out_shape=[
            jax.ShapeDtypeStruct((b, t, d), F32),
            jax.ShapeDtypeStruct((b, WINDOW, D_KV), F32),
            jax.ShapeDtypeStruct((b, WINDOW, D_KV), F32),
        ],
        scratch_shapes=[
            pltpu.VMEM((N_KV, WINDOW + tm, D_KV), BF16),
            pltpu.VMEM((N_KV, WINDOW + tm, D_KV), BF16),
            pltpu.VMEM((GROUP, tm, D_KV), BF16),
            pltpu.VMEM((tm, D_ATTN), F32),
        ],
        compiler_params=pltpu.CompilerParams(
            dimension_semantics=("arbitrary", "arbitrary"), vmem_limit_bytes=VMEM_LIMIT),
        name="mixer_b_prompt",
    )(sinks, x, gkv, gb, wkv, wq, wz, kg, qg, seg, bias, wout)


def _mixer_b_sample_kernel(x_ref, ck_ref, cv_ref, gkv_ref, gb_ref, wkv_ref, wq_ref, wz_ref, kg_ref,
                           qg_ref, seg_ref, bias_ref, sink_ref, wout_ref,
                           y_ref, nk_ref, nv_ref,
                           kc_ref, vc_ref, q_ref, o_ref):
    nb, ts, d = x_ref.shape
    x = x_ref[...].reshape(nb * ts, d)
    k, v, q, zg = _project_kvqz(x, gkv_ref, gb_ref, wkv_ref, wq_ref, wz_ref, kg_ref, qg_ref, seg_ref)
    for g in range(GROUP):
        q_ref[g] = q[g].reshape(nb, ts, D_KV)
    ctx = kc_ref.shape[1]
    kc_ref[:, 0:WINDOW, :] = ck_ref[...]
    vc_ref[:, 0:WINDOW, :] = cv_ref[...]
    kc_ref[:, WINDOW:WINDOW + ts, :] = k.reshape(nb, ts, D_KV)
    vc_ref[:, WINDOW:WINDOW + ts, :] = v.reshape(nb, ts, D_KV)
    kc_ref[:, WINDOW + ts:ctx, :] = jnp.zeros((nb, ctx - WINDOW - ts, D_KV), F32)
    vc_ref[:, WINDOW + ts:ctx, :] = jnp.zeros((nb, ctx - WINDOW - ts, D_KV), F32)
    nk_ref[...] = kc_ref[:, ts:ts + WINDOW, :]
    nv_ref[...] = vc_ref[:, ts:ts + WINDOW, :]

    masks = [_lane_segment_mask(kvh) for kvh in range(N_KV)]
    bias = bias_ref[...]
    sink = sink_ref[...]

    def seq(i, carry):
        kk = kc_ref[i].astype(BF16)
        vv = vc_ref[i].astype(BF16)
        rows = []
        for g in range(GROUP):
            qg_rows = q_ref[g, i]
            for kvh in range(N_KV):
                rows.append(jnp.where(masks[kvh], qg_rows, jnp.zeros_like(qg_rows)))
        lhs = jnp.concatenate(rows, axis=0).astype(BF16)
        s = _dot_nt(lhs, kk) + bias
        p, denom = _softmax_unnorm(s, sink)
        o = _dot(p.astype(BF16), vv) / denom
        for g in range(GROUP):
            acc = None
            for kvh in range(N_KV):
                r = (g * N_KV + kvh) * ts
                part = jnp.where(masks[kvh], o[r:r + ts, :], 0.0)
                acc = part if acc is None else acc + part
            o_ref[i, :, g * D_KV:(g + 1) * D_KV] = acc
        return carry

    lax.fori_loop(0, nb, seq, 0)

    og = (o_ref[...].reshape(nb * ts, D_ATTN) * zg).astype(BF16)
    y_ref[...] = (x + _dot(og, wout_ref[...])).reshape(nb, ts, d)


def _mixer_b_sample(x, ck, cv, gkv, gb, wkv, wq, wz, kg, qg, seg, bias, sink, wout):
    b, ts, d = x.shape
    nb = SAMPLE_BLOCK_ATTN
    const = lambda shape: pl.BlockSpec(shape, lambda i: (0,) * len(shape))
    rows = N_HEADS * ts
    ctx = bias.shape[1]
    return pl.pallas_call(
        _mixer_b_sample_kernel,
        grid=(b // nb,),
        in_specs=[
            pl.BlockSpec((nb, ts, d), lambda i: (i, 0, 0)),
            pl.BlockSpec((nb, WINDOW, D_KV), lambda i: (i, 0, 0)),
            pl.BlockSpec((nb, WINDOW, D_KV), lambda i: (i, 0, 0)),
            const((1, d)),
            const((1, d)),
            const((d, 2 * D_KV)),
            const((d, D_ATTN)),
            const((d, D_ATTN)),
            const((1, D_KV)),
            const((1, D_KV)),
            const((D_KV, D_KV)),
            const((rows, ctx)),
            const((rows, 1)),
            const((D_ATTN, d)),
        ],
        out_specs=[
            pl.BlockSpec((nb, ts, d), lambda i: (i, 0, 0)),
            pl.BlockSpec((nb, WINDOW, D_KV), lambda i: (i, 0, 0)),
            pl.BlockSpec((nb, WINDOW, D_KV), lambda i: (i, 0, 0)),
        ],
        out_shape=[
            jax.ShapeDtypeStruct((b, ts, d), F32),
            jax.ShapeDtypeStruct((b, WINDOW, D_KV), F32),
            jax.ShapeDtypeStruct((b, WINDOW, D_KV), F32),
        ],
        scratch_shapes=[
            pltpu.VMEM((nb, ctx, D_KV), F32),
            pltpu.VMEM((nb, ctx, D_KV), F32),
            pltpu.VMEM((GROUP, nb, ts, D_KV), F32),
            pltpu.VMEM((nb, ts, D_ATTN), F32),
        ],
        compiler_params=pltpu.CompilerParams(
            dimension_semantics=("arbitrary",), vmem_limit_bytes=VMEM_LIMIT),
        name="mixer_b_sample",
    )(x, ck, cv, gkv, gb, wkv, wq, wz, kg, qg, seg, bias, sink, wout)


def _alibi_slopes():
    return 2.0 ** (-8.0 * np.arange(1, N_HEADS + 1) / N_HEADS)


def _band_bias(n_query, n_key, first_block):
    dist = np.arange(n_query)[:, None] + WINDOW - np.arange(n_key)[None, :]
    allowed = (dist >= 0) & (dist < WINDOW)
    if first_block:
        allowed = allowed & (np.arange(n_key)[None, :] >= WINDOW)
    bias = -_alibi_slopes()[:, None, None] * dist[None].astype(np.float64)
    return np.where(allowed[None], bias, -np.inf).astype(np.float32)


def _group_major(w, axis):
    shape = w.shape
    w = w.reshape(shape[:axis] + (N_KV, GROUP, HEAD_DIM) + shape[axis + 1:])
    w = jnp.swapaxes(w, axis, axis + 1)
    return w.reshape(shape)


def kernel(x_prompt, x_sample, state_conv, cache_k, cache_v, norm_a, w_in_a, w_dw_a, b_dw_a, ln_g_a, ln_b_a, w_out_a, norm_kv, w_kv, k_norm, norm_b, w_in_b, q_norm, sinks_b, w_out_b):
    assert norm_a.shape[0] == 1 and norm_b.shape[0] == 1
    row = lambda v: v.reshape(1, -1).astype(F32)

    a_args = (row(norm_a[0]), w_in_a[0].astype(BF16), w_dw_a[0].astype(F32), row(b_dw_a[0]),
              row(ln_g_a[0]), row(ln_b_a[0]), w_out_a[0].astype(BF16))
    xp, st_p = _mixer_a_prompt(x_prompt, *a_args)
    xs, st_s = _mixer_a_sample(x_sample, state_conv, *a_args)

    wq = _group_major(w_in_b[0][:, :D_ATTN], 1).astype(BF16)
    wz = _group_major(w_in_b[0][:, D_ATTN:], 1).astype(BF16)
    wout = _group_major(w_out_b[0], 0).astype(BF16)
    sinks = sinks_b[0].astype(F32)
    kg = row(jnp.tile(k_norm, N_KV))
    qg = row(jnp.tile(q_norm[0], N_KV)) * (HEAD_DIM ** -0.5)
    seg = jnp.asarray(np.kron(np.eye(N_KV), np.full((HEAD_DIM, HEAD_DIM), 1.0 / HEAD_DIM)), BF16)
    shared = (row(norm_kv), row(norm_b[0]), w_kv.astype(BF16), wq, wz, kg, qg, seg)

    span = WINDOW + Q_BLOCK
    bias_p = jnp.asarray(np.stack([_band_bias(Q_BLOCK, span, True), _band_bias(Q_BLOCK, span, False)]))
    yp, k_tail, v_tail = _mixer_b_prompt(xp, sinks, *shared, bias_p, wout)

    ts = x_sample.shape[1]
    head_of_row = np.array([kvh * GROUP + g for g in range(GROUP) for kvh in range(N_KV)])
    ctx = -(-(WINDOW + ts) // BF16_ROWS) * BF16_ROWS
    bias_s = jnp.asarray(_band_bias(ts, ctx, False)[head_of_row].reshape(N_HEADS * ts, ctx))
    sink_s = jnp.repeat(sinks[head_of_row], ts).reshape(N_HEADS * ts, 1)
    nb = cache_k.shape[0]
    ys, nk, nv = _mixer_b_sample(xs, cache_k.reshape(nb, WINDOW, D_KV), cache_v.reshape(nb, WINDOW, D_KV),
                                 *shared, bias_s, sink_s, wout)

    kv4 = lambda a: a.reshape(a.shape[0], WINDOW, N_KV, HEAD_DIM)
    return (yp, ys, st_p[None, :, PREFIX_SKIP:, :], st_s, kv4(k_tail), kv4(v_tail), kv4(nk), kv4(nv))
```

```python
import functools

import numpy as np
import jax
import jax.numpy as jnp
from jax import lax
from jax.experimental import pallas as pl
from jax.experimental.pallas import tpu as pltpu

F32 = jnp.float32
BF16 = jnp.bfloat16

D_MODEL = 1024
D_CONV = 1024
CONV_WIDTH = 31
CONV_STATE = CONV_WIDTH - 1
N_HEADS = 16
N_KV = 4
GROUP = N_HEADS // N_KV
HEAD_DIM = 64
D_ATTN = N_HEADS * HEAD_DIM
D_KV = N_KV * HEAD_DIM
WINDOW = 128
Q_BLOCK = 128
RMS_EPS = 1e-6
LN_EPS = 1e-5

SUBLANES = 8
LANES = 128
LANE_CHUNKS = D_CONV // LANES
assert LANE_CHUNKS == SUBLANES
BF16_ROWS = 16
PAIR_ROWS = 2 * LANE_CHUNKS
assert PAIR_ROWS == BF16_ROWS
PREFIX_ROWS = 32
PREFIX_SKIP = PREFIX_ROWS - CONV_STATE
TILE_M = 512
CONV_ROWS = 32
SAMPLE_BLOCK = 32
SAMPLE_BLOCK_ATTN = 16
SEQ_UNROLL = 4
VMEM_LIMIT = 56 * 1024 * 1024


def _dot(a, b):
    return jnp.dot(a, b, preferred_element_type=F32)


def _dot_nt(a, b):
    return lax.dot_general(a, b, (((1,), (1,)), ((), ())), preferred_element_type=F32)


def _rms_unit(x):
    return x * lax.rsqrt(jnp.mean(x * x, axis=-1, keepdims=True) + RMS_EPS)


def _silu(x):
    return x * jax.nn.sigmoid(x)


def _head_rms(x, seg_mean, gain):
    ms = _dot((x * x).astype(BF16), seg_mean)
    return x * lax.rsqrt(ms + RMS_EPS) * gain


def _conv_ln_gate(tap, wdw_ref, bdw, lng, lnb, zgate):
    acc = tap(0) * wdw_ref[0:1, :] + bdw
    for k in range(1, CONV_WIDTH):
        acc = acc + tap(k) * wdw_ref[k:k + 1, :]
    mu = jnp.mean(acc, axis=-1, keepdims=True)
    xc = acc - mu
    var = jnp.mean(xc * xc, axis=-1, keepdims=True)
    y = xc * lax.rsqrt(var + LN_EPS) * lng + lnb
    return _silu(y) * zgate


def _mixer_a_prompt_kernel(x_ref, g_ref, win_ref, w16_ref, bdw_ref, lng_ref, lnb_ref, wout_ref,
                           y_ref, st_ref, hist_ref, hprev_ref, even_ref, odd_ref, tmp_ref, zg_ref, act_ref):
    t = pl.program_id(1)
    tm = x_ref.shape[1]
    prefix = PREFIX_ROWS * LANE_CHUNKS
    total = prefix + tm * LANE_CHUNKS

    hist_ref[0:prefix, :] = jnp.where(t > 0, hprev_ref[...], jnp.zeros((prefix, LANES), F32))

    x = x_ref[0]
    h = (_rms_unit(x) * g_ref[...]).astype(BF16)
    a = _dot(h, win_ref[:, 0:D_CONV])
    gl = _dot(h, win_ref[:, D_CONV:2 * D_CONV])
    v = a * jax.nn.sigmoid(gl)
    for c in range(LANE_CHUNKS):
        hist_ref[pl.ds(prefix + c, tm, stride=LANE_CHUNKS), :] = v[:, c * LANES:(c + 1) * LANES]
    zg_ref[...] = _silu(_dot(h, win_ref[:, 2 * D_CONV:3 * D_CONV]))

    @pl.when(t == pl.num_programs(1) - 1)
    def _():
        st_ref[0] = v[tm - PREFIX_ROWS:tm, :]

    even_ref[...] = hist_ref[...].astype(BF16)
    odd_ref[...] = hist_ref[LANE_CHUNKS:total - LANE_CHUNKS, :].astype(BF16)

    lng = lng_ref[...]
    lnb = lnb_ref[...]
    rows = CONV_ROWS * LANE_CHUNKS
    for i in range(tm // CONV_ROWS):
        r0 = i * CONV_ROWS
        acc = None
        for k in range(CONV_WIDTH):
            tau = r0 + PREFIX_SKIP + k
            src, row = (even_ref, tau * LANE_CHUNKS) if tau % 2 == 0 else (odd_ref, (tau - 1) * LANE_CHUNKS)
            tap = src[row:row + rows, :].astype(F32).reshape(CONV_ROWS // 2, PAIR_ROWS, LANES)
            term = tap * w16_ref[k * PAIR_ROWS:(k + 1) * PAIR_ROWS, :].astype(F32)
            acc = term if acc is None else acc + term
        acc = acc.reshape(CONV_ROWS, LANE_CHUNKS, LANES) + bdw_ref[...]
        tmp_ref[i * rows:(i + 1) * rows, :] = acc.reshape(rows, LANES)
        conv = jnp.concatenate(
            [tmp_ref[pl.ds(i * rows + c, CONV_ROWS, stride=LANE_CHUNKS), :] for c in range(LANE_CHUNKS)], axis=1)
        mu = jnp.mean(conv, axis=-1, keepdims=True)
        xc = conv - mu
        var = jnp.mean(xc * xc, axis=-1, keepdims=True)
        y = xc * lax.rsqrt(var + LN_EPS) * lng + lnb
        act_ref[r0:r0 + CONV_ROWS, :] = (_silu(y) * zg_ref[r0:r0 + CONV_ROWS, :]).astype(BF16)

    y_ref[0] = x + _dot(act_ref[...], wout_ref[...])

    hprev_ref[...] = hist_ref[tm * LANE_CHUNKS:total, :]


def _mixer_a_prompt(x, g, w_in, w_dw, b_dw, ln_g, ln_b, w_out):
    b, t, d = x.shape
    tm = TILE_M
    const = lambda shape: pl.BlockSpec(shape, lambda i, j: (0,) * len(shape))
    w_pairs = jnp.tile(w_dw.reshape(CONV_WIDTH, 1, LANE_CHUNKS, LANES), (1, 2, 1, 1))
    w_pairs = w_pairs.reshape(CONV_WIDTH * PAIR_ROWS, LANES).astype(BF16)
    return pl.pallas_call(
        _mixer_a_prompt_kernel,
        grid=(b, t // tm),
        in_specs=[
            pl.BlockSpec((1, tm, d), lambda i, j: (i, j, 0)),
            const((1, d)),
            const((d, 3 * D_CONV)),
            const((CONV_WIDTH * PAIR_ROWS, LANES)),
            const((LANE_CHUNKS, LANES)),
            const((1, D_CONV)),
            const((1, D_CONV)),
            const((D_CONV, d)),
        ],
        out_specs=[
            pl.BlockSpec((1, tm, d), lambda i, j: (i, j, 0)),
            pl.BlockSpec((1, PREFIX_ROWS, D_CONV), lambda i, j: (i, 0, 0)),
        ],
        out_shape=[
            jax.ShapeDtypeStruct((b, t, d), F32),
            jax.ShapeDtypeStruct((b, PREFIX_ROWS, D_CONV), F32),
        ],
        scratch_shapes=[
            pltpu.VMEM(((PREFIX_ROWS + tm) * LANE_CHUNKS, LANES), F32),
            pltpu.VMEM((PREFIX_ROWS * LANE_CHUNKS, LANES), F32),
            pltpu.VMEM(((PREFIX_ROWS + tm) * LANE_CHUNKS, LANES), BF16),
            pltpu.VMEM(((PREFIX_ROWS + tm - 2) * LANE_CHUNKS, LANES), BF16),
            pltpu.VMEM((tm * LANE_CHUNKS, LANES), F32),
            pltpu.VMEM((tm, D_CONV), F32),
            pltpu.VMEM((tm, D_CONV), BF16),
        ],
        compiler_params=pltpu.CompilerParams(
            dimension_semantics=("arbitrary", "arbitrary"), vmem_limit_bytes=VMEM_LIMIT),
        name="mixer_a_prompt",
    )(x, g, w_in, w_pairs, b_dw.reshape(LANE_CHUNKS, LANES), ln_g, ln_b, w_out)


def _mixer_a_sample_kernel(x_ref, st_ref, g_ref, win_ref, wdw_ref, bdw_ref, lng_ref, lnb_ref, wout_ref,
                           y_ref, nst_ref, vf_ref, zg_ref, act_ref):
    nb, ts, d = x_ref.shape
    x = x_ref[...].reshape(nb * ts, d)
    h = (_rms_unit(x) * g_ref[...]).astype(BF16)
    a = _dot(h, win_ref[:, 0:D_CONV])
    gl = _dot(h, win_ref[:, D_CONV:2 * D_CONV])
    v = a * jax.nn.sigmoid(gl)
    zg_ref[...] = _silu(_dot(h, win_ref[:, 2 * D_CONV:3 * D_CONV])).reshape(nb, ts, D_CONV)
    vf_ref[:, 0:CONV_STATE, :] = st_ref[0]
    vf_ref[:, CONV_STATE:CONV_STATE + ts, :] = v.reshape(nb, ts, D_CONV)
    nst_ref[0] = vf_ref[:, ts:ts + CONV_STATE, :]

    bdw = bdw_ref[...]
    lng = lng_ref[...]
    lnb = lnb_ref[...]
    cb = CONV_ROWS // ts

    def chunk(i, carry):
        b0 = pl.multiple_of(i * cb, cb)

        def tap(k):
            return vf_ref[pl.ds(b0, cb), k:k + ts, :].reshape(cb * ts, D_CONV)

        zgate = zg_ref[pl.ds(b0, cb), :, :].reshape(cb * ts, D_CONV)
        gated = _conv_ln_gate(tap, wdw_ref, bdw, lng, lnb, zgate)
        act_ref[pl.ds(b0, cb), :, :] = gated.astype(BF16).reshape(cb, ts, D_CONV)
        return carry

    lax.fori_loop(0, nb // cb, chunk, 0, unroll=2)

    out = _dot(act_ref[...].reshape(nb * ts, D_CONV), wout_ref[...])
    y_ref[...] = (x + out).reshape(nb, ts, d)


def _mixer_a_sample(x, state, g, w_in, w_dw, b_dw, ln_g, ln_b, w_out):
    b, ts, d = x.shape
    nb = SAMPLE_BLOCK
    const = lambda shape: pl.BlockSpec(shape, lambda i: (0,) * len(shape))
    return pl.pallas_call(
        _mixer_a_sample_kernel,
        grid=(b // nb,),
        in_specs=[
            pl.BlockSpec((nb, ts, d), lambda i: (i, 0, 0)),
            pl.BlockSpec((1, nb, CONV_STATE, D_CONV), lambda i: (0, i, 0, 0)),
            const((1, d)),
            const((d, 3 * D_CONV)),
            const((CONV_WIDTH, D_CONV)),
            const((1, D_CONV)),
            const((1, D_CONV)),
            const((1, D_CONV)),
            const((D_CONV, d)),
        ],
        out_specs=[
            pl.BlockSpec((nb, ts, d), lambda i: (i, 0, 0)),
            pl.BlockSpec((1, nb, CONV_STATE, D_CONV), lambda i: (0, i, 0, 0)),
        ],
        out_shape=[
            jax.ShapeDtypeStruct((b, ts, d), F32),
            jax.ShapeDtypeStruct((1, b, CONV_STATE, D_CONV), F32),
        ],
        scratch_shapes=[
            pltpu.VMEM((nb, CONV_STATE + ts + 2, D_CONV), F32),
            pltpu.VMEM((nb, ts, D_CONV), F32),
            pltpu.VMEM((nb, ts, D_CONV), BF16),
        ],
        compiler_params=pltpu.CompilerParams(
            dimension_semantics=("arbitrary",), vmem_limit_bytes=VMEM_LIMIT),
        name="mixer_a_sample",
    )(x, state, g, w_in, w_dw, b_dw, ln_g, ln_b, w_out)


def _project_kvqz(x, gkv_ref, gb_ref, wkv_ref, wq_ref, wz_ref, kg_ref, qg_ref, seg_ref):
    xn = _rms_unit(x)
    hk = (xn * gkv_ref[...]).astype(BF16)
    hq = (xn * gb_ref[...]).astype(BF16)
    seg = seg_ref[...]
    k = _head_rms(_dot(hk, wkv_ref[:, 0:D_KV]), seg, kg_ref[...])
    v = _dot(hk, wkv_ref[:, D_KV:2 * D_KV])
    q = []
    for g in range(GROUP):
        cols = slice(g * D_KV, (g + 1) * D_KV)
        q.append(_head_rms(_dot(hq, wq_ref[:, cols]), seg, qg_ref[...]))
    zg = _silu(_dot(hq, wz_ref[...]))
    return k, v, q, zg


def _lane_segment_mask(kvh):
    lane = lax.broadcasted_iota(jnp.int32, (1, D_KV), 1)
    return (lane >= kvh * HEAD_DIM) & (lane < (kvh + 1) * HEAD_DIM)


def _softmax_unnorm(s, sink):
    m = jnp.maximum(jnp.max(s, axis=-1, keepdims=True), sink)
    p = jnp.exp(s - m)
    denom = jnp.sum(p, axis=-1, keepdims=True) + jnp.exp(sink - m)
    return p, denom


def _mixer_b_prompt_kernel(sink_ref, x_ref, gkv_ref, gb_ref, wkv_ref, wq_ref, wz_ref, kg_ref, qg_ref,
                           seg_ref, bias_ref, wout_ref,
                           y_ref, kt_ref, vt_ref,
                           km_ref, vm_ref, kprev_ref, vprev_ref, q_ref, o_ref):
    t = pl.program_id(1)
    tm = x_ref.shape[1]

    zeros = jnp.zeros((N_KV, WINDOW, D_KV), BF16)
    km_ref[:, 0:WINDOW, :] = jnp.where(t > 0, kprev_ref[...], zeros)
    vm_ref[:, 0:WINDOW, :] = jnp.where(t > 0, vprev_ref[...], zeros)

    x = x_ref[0]
    k, v, q, zg = _project_kvqz(x, gkv_ref, gb_ref, wkv_ref, wq_ref, wz_ref, kg_ref, qg_ref, seg_ref)
    for g in range(GROUP):
        q_ref[g] = q[g].astype(BF16)
    for kvh in range(N_KV):
        mask = _lane_segment_mask(kvh)
        km_ref[kvh, WINDOW:WINDOW + tm, :] = jnp.where(mask, k, 0.0).astype(BF16)
        vm_ref[kvh, WINDOW:WINDOW + tm, :] = jnp.where(mask, v, 0.0).astype(BF16)

    @pl.when(t == pl.num_programs(1) - 1)
    def _():
        kt_ref[0] = k[tm - WINDOW:tm, :]
        vt_ref[0] = v[tm - WINDOW:tm, :]

    span = WINDOW + Q_BLOCK

    for j in range(tm // Q_BLOCK):
        r0 = j * Q_BLOCK
        bsel = jnp.where(t == 0, 0, 1) if j == 0 else 1
        qstack = jnp.concatenate([q_ref[g, r0:r0 + Q_BLOCK, :] for g in range(GROUP)], axis=0)
        out = None
        for kvh in range(N_KV):
            s = _dot_nt(qstack, km_ref[kvh, r0:r0 + span, :]) + bias_ref[bsel, kvh]
            probs = []
            for g in range(GROUP):
                p, denom = _softmax_unnorm(s[g * Q_BLOCK:(g + 1) * Q_BLOCK, :], sink_ref[kvh * GROUP + g])
                probs.append((p * (1.0 / denom)).astype(BF16))
            o = _dot(jnp.concatenate(probs, axis=0), vm_ref[kvh, r0:r0 + span, :])
            out = o if out is None else out + o
        for g in range(GROUP):
            o_ref[r0:r0 + Q_BLOCK, g * D_KV:(g + 1) * D_KV] = out[g * Q_BLOCK:(g + 1) * Q_BLOCK, :]

    y_ref[0] = x + _dot((o_ref[...] * zg).astype(BF16), wout_ref[...])

    for kvh in range(N_KV):
        mask = _lane_segment_mask(kvh)
        kprev_ref[kvh] = jnp.where(mask, k[tm - WINDOW:tm, :], 0.0).astype(BF16)
        vprev_ref[kvh] = jnp.where(mask, v[tm - WINDOW:tm, :], 0.0).astype(BF16)


def _mixer_b_prompt(x, sinks, gkv, gb, wkv, wq, wz, kg, qg, seg, bias, wout):
    b, t, d = x.shape
    tm = TILE_M
    const = lambda shape: pl.BlockSpec(shape, lambda i, j: (0,) * len(shape))
    return pl.pallas_call(
        _mixer_b_prompt_kernel,
        grid=(b, t // tm),
        in_specs=[
            pl.BlockSpec(memory_space=pltpu.SMEM),
            pl.BlockSpec((1, tm, d), lambda i, j: (i, j, 0)),
            const((1, d)),
            const((1, d)),
            const((d, 2 * D_KV)),
            const((d, D_ATTN)),
            const((d, D_ATTN)),
            const((1, D_KV)),
            const((1, D_KV)),
            const((D_KV, D_KV)),
            const((2, N_KV, GROUP * Q_BLOCK, WINDOW + Q_BLOCK)),
            const((D_ATTN, d)),
        ],
        out_specs=[
            pl.BlockSpec((1, tm, d), lambda i, j: (i, j, 0)),
            pl.BlockSpec((1, WINDOW, D_KV), lambda i, j: (i, 0, 0)),
            pl.BlockSpec((1, WINDOW, D_KV), lambda i, j: (i, 0, 0)),
        ],
        out_shape=[
            jax.ShapeDtypeStruct((b, t, d), F32),
            jax.ShapeDtypeStruct((b, WINDOW, D_KV), F32),
            jax.ShapeDtypeStruct((b, WINDOW, D_KV), F32),
        ],
        scratch_shapes=[
            pltpu.VMEM((N_KV, WINDOW + tm, D_KV), BF16),
            pltpu.VMEM((N_KV, WINDOW + tm, D_KV), BF16),
            pltpu.VMEM((N_KV, WINDOW, D_KV), BF16),
            pltpu.VMEM((N_KV, WINDOW, D_KV), BF16),
            pltpu.VMEM((GROUP, tm, D_KV), BF16),
            pltpu.VMEM((tm, D_ATTN), F32),
        ],
        compiler_params=pltpu.CompilerParams(
            dimension_semantics=("arbitrary", "arbitrary"), vmem_limit_bytes=VMEM_LIMIT),
        name="mixer_b_prompt",
    )(sinks, x, gkv, gb, wkv, wq, wz, kg, qg, seg, bias, wout)


def _mixer_b_sample_kernel(x_ref, ck_ref, cv_ref, gkv_ref, gb_ref, wkv_ref, wq_ref, wz_ref, kg_ref,
                           qg_ref, seg_ref, bias_ref, sink_ref, wout_ref,
                           y_ref, nk_ref, nv_ref,
                           kc_ref, vc_ref, q_ref, o_ref):
    nb, ts, d = x_ref.shape
    x = x_ref[...].reshape(nb * ts, d)
    k, v, q, zg = _project_kvqz(x, gkv_ref, gb_ref, wkv_ref, wq_ref, wz_ref, kg_ref, qg_ref, seg_ref)
    for g in range(GROUP):
        q_ref[g] = q[g].reshape(nb, ts, D_KV)
    ctx = kc_ref.shape[1]
    kc_ref[:, 0:WINDOW, :] = ck_ref[...]
    vc_ref[:, 0:WINDOW, :] = cv_ref[...]
    kc_ref[:, WINDOW:WINDOW + ts, :] = k.reshape(nb, ts, D_KV)
    vc_ref[:, WINDOW:WINDOW + ts, :] = v.reshape(nb, ts, D_KV)
    kc_ref[:, WINDOW + ts:ctx, :] = jnp.zeros((nb, ctx - WINDOW - ts, D_KV), F32)
    vc_ref[:, WINDOW + ts:ctx, :] = jnp.zeros((nb, ctx - WINDOW - ts, D_KV), F32)
    nk_ref[...] = kc_ref[:, ts:ts + WINDOW, :]
    nv_ref[...] = vc_ref[:, ts:ts + WINDOW, :]

    masks = [_lane_segment_mask(kvh) for kvh in range(N_KV)]
    bias = bias_ref[...]
    sink = sink_ref[...]

    def seq(i, carry):
        kk = kc_ref[i].astype(BF16)
        vv = vc_ref[i].astype(BF16)
        rows = []
        for g in range(GROUP):
            qg_rows = q_ref[g, i]
            for kvh in range(N_KV):
                rows.append(jnp.where(masks[kvh], qg_rows, jnp.zeros_like(qg_rows)))
        lhs = jnp.concatenate(rows, axis=0).astype(BF16)
        s = _dot_nt(lhs, kk) + bias
        p, denom = _softmax_unnorm(s, sink)
        o = _dot((p * (1.0 / denom)).astype(BF16), vv)
        for g in range(GROUP):
            acc = None
            for kvh in range(N_KV):
                r = (g * N_KV + kvh) * ts
                part = jnp.where(masks[kvh], o[r:r + ts, :], 0.0)
                acc = part if acc is None else acc + part
            o_ref[i, :, g * D_KV:(g + 1) * D_KV] = acc
        return carry

    lax.fori_loop(0, nb, seq, 0, unroll=SEQ_UNROLL)

    og = (o_ref[...].reshape(nb * ts, D_ATTN) * zg).astype(BF16)
    y_ref[...] = (x + _dot(og, wout_ref[...])).reshape(nb, ts, d)


def _mixer_b_sample(x, ck, cv, gkv, gb, wkv, wq, wz, kg, qg, seg, bias, sink, wout):
    b, ts, d = x.shape
    nb = SAMPLE_BLOCK_ATTN
    const = lambda shape: pl.BlockSpec(shape, lambda i: (0,) * len(shape))
    rows = N_HEADS * ts
    ctx = bias.shape[1]
    return pl.pallas_call(
        _mixer_b_sample_kernel,
        grid=(b // nb,),
        in_specs=[
            pl.BlockSpec((nb, ts, d), lambda i: (i, 0, 0)),
            pl.BlockSpec((nb, WINDOW, D_KV), lambda i: (i, 0, 0)),
            pl.BlockSpec((nb, WINDOW, D_KV), lambda i: (i, 0, 0)),
            const((1, d)),
            const((1, d)),
            const((d, 2 * D_KV)),
            const((d, D_ATTN)),
            const((d, D_ATTN)),
            const((1, D_KV)),
            const((1, D_KV)),
            const((D_KV, D_KV)),
            const((rows, ctx)),
            const((rows, 1)),
            const((D_ATTN, d)),
        ],
        out_specs=[
            pl.BlockSpec((nb, ts, d), lambda i: (i, 0, 0)),
            pl.BlockSpec((nb, WINDOW, D_KV), lambda i: (i, 0, 0)),
            pl.BlockSpec((nb, WINDOW, D_KV), lambda i: (i, 0, 0)),
        ],
        out_shape=[
            jax.ShapeDtypeStruct((b, ts, d), F32),
            jax.ShapeDtypeStruct((b, WINDOW, D_KV), F32),
            jax.ShapeDtypeStruct((b, WINDOW, D_KV), F32),
        ],
        scratch_shapes=[
            pltpu.VMEM((nb, ctx, D_KV), F32),
            pltpu.VMEM((nb, ctx, D_KV), F32),
            pltpu.VMEM((GROUP, nb, ts, D_KV), F32),
            pltpu.VMEM((nb, ts, D_ATTN), F32),
        ],
        compiler_params=pltpu.CompilerParams(
            dimension_semantics=("arbitrary",), vmem_limit_bytes=VMEM_LIMIT),
        name="mixer_b_sample",
    )(x, ck, cv, gkv, gb, wkv, wq, wz, kg, qg, seg, bias, sink, wout)


def _alibi_slopes():
    return 2.0 ** (-8.0 * np.arange(1, N_HEADS + 1) / N_HEADS)


def _band_bias(n_query, n_key, first_block):
    dist = np.arange(n_query)[:, None] + WINDOW - np.arange(n_key)[None, :]
    allowed = (dist >= 0) & (dist < WINDOW)
    if first_block:
        allowed = allowed & (np.arange(n_key)[None, :] >= WINDOW)
    bias = -_alibi_slopes()[:, None, None] * dist[None].astype(np.float64)
    return np.where(allowed[None], bias, -np.inf).astype(np.float32)


def _group_major(w, axis):
    shape = w.shape
    w = w.reshape(shape[:axis] + (N_KV, GROUP, HEAD_DIM) + shape[axis + 1:])
    w = jnp.swapaxes(w, axis, axis + 1)
    return w.reshape(shape)


def kernel(x_prompt, x_sample, state_conv, cache_k, cache_v, norm_a, w_in_a, w_dw_a, b_dw_a, ln_g_a, ln_b_a, w_out_a, norm_kv, w_kv, k_norm, norm_b, w_in_b, q_norm, sinks_b, w_out_b):
    assert norm_a.shape[0] == 1 and norm_b.shape[0] == 1
    row = lambda v: v.reshape(1, -1).astype(F32)

    a_args = (row(norm_a[0]), w_in_a[0].astype(BF16), w_dw_a[0].astype(F32), row(b_dw_a[0]),
              row(ln_g_a[0]), row(ln_b_a[0]), w_out_a[0].astype(BF16))
    xp, st_p = _mixer_a_prompt(x_prompt, *a_args)
    xs, st_s = _mixer_a_sample(x_sample, state_conv, *a_args)

    wq = _group_major(w_in_b[0][:, :D_ATTN], 1).astype(BF16)
    wz = _group_major(w_in_b[0][:, D_ATTN:], 1).astype(BF16)
    wout = _group_major(w_out_b[0], 0).astype(BF16)
    sinks = sinks_b[0].astype(F32)
    kg = row(jnp.tile(k_norm, N_KV))
    qg = row(jnp.tile(q_norm[0], N_KV)) * (HEAD_DIM ** -0.5)
    seg = jnp.asarray(np.kron(np.eye(N_KV), np.full((HEAD_DIM, HEAD_DIM), 1.0 / HEAD_DIM)), BF16)
    shared = (row(norm_kv), row(norm_b[0]), w_kv.astype(BF16), wq, wz, kg, qg, seg)

    span = WINDOW + Q_BLOCK
    bias_p = jnp.asarray(np.stack([_band_bias(Q_BLOCK, span, True), _band_bias(Q_BLOCK, span, False)])
                         .reshape(2, N_KV, GROUP * Q_BLOCK, span))
    yp, k_tail, v_tail = _mixer_b_prompt(xp, sinks, *shared, bias_p, wout)

    ts = x_sample.shape[1]
    head_of_row = np.array([kvh * GROUP + g for g in range(GROUP) for kvh in range(N_KV)])
    ctx = -(-(WINDOW + ts) // BF16_ROWS) * BF16_ROWS
    bias_s = jnp.asarray(_band_bias(ts, ctx, False)[head_of_row].reshape(N_HEADS * ts, ctx))
    sink_s = jnp.repeat(sinks[head_of_row], ts).reshape(N_HEADS * ts, 1)
    nb = cache_k.shape[0]
    ys, nk, nv = _mixer_b_sample(xs, cache_k.reshape(nb, WINDOW, D_KV), cache_v.reshape(nb, WINDOW, D_KV),
                                 *shared, bias_s, sink_s, wout)

    kv4 = lambda a: a.reshape(a.shape[0], WINDOW, N_KV, HEAD_DIM)
    return (yp, ys, st_p[None, :, PREFIX_SKIP:, :], st_s, kv4(k_tail), kv4(v_tail), kv4(nk), kv4(nv))
```

```python
import functools

import numpy as np
import jax
import jax.numpy as jnp
from jax import lax
from jax.experimental import pallas as pl
from jax.experimental.pallas import tpu as pltpu

F32 = jnp.float32
BF16 = jnp.bfloat16

D_MODEL = 1024
D_CONV = 1024
CONV_WIDTH = 31
CONV_STATE = CONV_WIDTH - 1
N_HEADS = 16
N_KV = 4
GROUP = N_HEADS // N_KV
HEAD_DIM = 64
D_ATTN = N_HEADS * HEAD_DIM
D_KV = N_KV * HEAD_DIM
WINDOW = 128
Q_BLOCK = 128
RMS_EPS = 1e-6
LN_EPS = 1e-5
LOG2E = float(np.log2(np.e))

SUBLANES = 8
LANES = 128
LANE_CHUNKS = D_CONV // LANES
assert LANE_CHUNKS == SUBLANES
BF16_ROWS = 16
PAIR_ROWS = 2 * LANE_CHUNKS
assert PAIR_ROWS == BF16_ROWS
PREFIX_ROWS = 32
PREFIX_SKIP = PREFIX_ROWS - CONV_STATE
TILE_M = 512
SUB_M = 256
CONV_ROWS = 32
SAMPLE_BLOCK = 32
SAMPLE_BLOCK_ATTN = 16
SEQ_UNROLL = 4
VMEM_LIMIT = 56 * 1024 * 1024


def _dot(a, b):
    return jnp.dot(a, b, preferred_element_type=F32)


def _dot_nt(a, b):
    return lax.dot_general(a, b, (((1,), (1,)), ((), ())), preferred_element_type=F32)


def _rms_unit(x):
    return x * lax.rsqrt(jnp.mean(x * x, axis=-1, keepdims=True) + RMS_EPS)


def _silu(x):
    return x * jax.nn.sigmoid(x)


def _head_rms(x, seg_mean, gain):
    ms = _dot((x * x).astype(BF16), seg_mean)
    return x * lax.rsqrt(ms + RMS_EPS) * gain


def _conv_ln_gate(tap, wdw_ref, bdw, lng, lnb, zgate):
    acc = tap(0) * wdw_ref[0:1, :] + bdw
    for k in range(1, CONV_WIDTH):
        acc = acc + tap(k) * wdw_ref[k:k + 1, :]
    mu = jnp.mean(acc, axis=-1, keepdims=True)
    xc = acc - mu
    var = jnp.mean(xc * xc, axis=-1, keepdims=True)
    y = xc * lax.rsqrt(var + LN_EPS) * lng + lnb
    return _silu(y) * zgate


def _interleave(first, second):
    if not first or not second:
        for f in first + second:
            f()
        return
    long, short = (first, second) if len(first) >= len(second) else (second, first)
    done = 0
    for i, f in enumerate(long):
        f()
        want = (i + 1) * len(short) // len(long)
        while done < want:
            short[done]()
            done += 1


def _mixer_a_prompt_kernel(x_ref, g_ref, win_ref, w16_ref, bdw_ref, lng_ref, lnb_ref, wout_ref,
                           y_ref, st_ref, hist_ref, hprev_ref, even_ref, odd_ref, tmp_ref, h_ref, zg_ref,
                           act_ref):
    t = pl.program_id(1)
    tm = x_ref.shape[1]
    prefix = PREFIX_ROWS * LANE_CHUNKS
    total = prefix + tm * LANE_CHUNKS

    hist_ref[0:prefix, :] = jnp.where(t > 0, hprev_ref[...], jnp.zeros((prefix, LANES), F32))

    lng = lng_ref[...]
    lnb = lnb_ref[...]
    rows = CONV_ROWS * LANE_CHUNKS
    n_sub = tm // SUB_M
    half = D_CONV // 2

    def project_items(c):
        lo, hi = c * SUB_M, (c + 1) * SUB_M

        def norm():
            h_ref[lo:hi, :] = (_rms_unit(x_ref[0, lo:hi, :]) * g_ref[...]).astype(BF16)

        def glu(n):
            def run():
                h = h_ref[lo:hi, :]
                a = _dot(h, win_ref[:, n * half:(n + 1) * half])
                gl = _dot(h, win_ref[:, D_CONV + n * half:D_CONV + (n + 1) * half])
                v = a * jax.nn.sigmoid(gl)
                for ch in range(half // LANES):
                    dst = prefix + lo * LANE_CHUNKS + n * (half // LANES) + ch
                    hist_ref[pl.ds(dst, SUB_M, stride=LANE_CHUNKS), :] = v[:, ch * LANES:(ch + 1) * LANES]
                if c == n_sub - 1:
                    st_ref[0, :, n * half:(n + 1) * half] = v[SUB_M - PREFIX_ROWS:SUB_M, :]
            return run

        def gate(n):
            def run():
                z = _dot(h_ref[lo:hi, :], win_ref[:, 2 * D_CONV + n * half:2 * D_CONV + (n + 1) * half])
                zg_ref[lo:hi, n * half:(n + 1) * half] = _silu(z)
            return run

        return [norm, glu(0), gate(0), glu(1), gate(1)]

    def pair_copies(c):
        lo, hi = c * SUB_M, (c + 1) * SUB_M
        e_lo = 0 if c == 0 else prefix + lo * LANE_CHUNKS
        e_hi = prefix + hi * LANE_CHUNKS
        even_ref[e_lo:e_hi, :] = hist_ref[e_lo:e_hi, :].astype(BF16)
        o_lo = 0 if c == 0 else e_lo - PAIR_ROWS
        o_hi = e_hi - PAIR_ROWS
        odd_ref[o_lo:o_hi, :] = hist_ref[o_lo + LANE_CHUNKS:o_hi + LANE_CHUNKS, :].astype(BF16)

    def conv_items(c):
        def chunk(i):
            def run():
                r0 = i * CONV_ROWS
                acc = None
                for k in range(CONV_WIDTH):
                    tau = r0 + PREFIX_SKIP + k
                    src, row = ((even_ref, tau * LANE_CHUNKS) if tau % 2 == 0
                                else (odd_ref, (tau - 1) * LANE_CHUNKS))
                    tap = src[row:row + rows, :].astype(F32).reshape(CONV_ROWS // 2, PAIR_ROWS, LANES)
                    term = tap * w16_ref[k * PAIR_ROWS:(k + 1) * PAIR_ROWS, :].astype(F32)
                    acc = term if acc is None else acc + term
                acc = acc.reshape(CONV_ROWS, LANE_CHUNKS, LANES) + bdw_ref[...]
                tmp_ref[i * rows:(i + 1) * rows, :] = acc.reshape(rows, LANES)
                conv = jnp.concatenate(
                    [tmp_ref[pl.ds(i * rows + ch, CONV_ROWS, stride=LANE_CHUNKS), :] for ch in range(LANE_CHUNKS)],
                    axis=1)
                mu = jnp.mean(conv, axis=-1, keepdims=True)
                xc = conv - mu
                var = jnp.mean(xc * xc, axis=-1, keepdims=True)
                y = xc * lax.rsqrt(var + LN_EPS) * lng + lnb
                act_ref[r0:r0 + CONV_ROWS, :] = (_silu(y) * zg_ref[r0:r0 + CONV_ROWS, :]).astype(BF16)
            return run
        return [chunk(i) for i in range(c * SUB_M // CONV_ROWS, (c + 1) * SUB_M // CONV_ROWS)]

    def output_items(c):
        lo, hi = c * SUB_M, (c + 1) * SUB_M

        def piece(n):
            def run():
                cols = slice(n * half, (n + 1) * half)
                y_ref[0, lo:hi, cols] = x_ref[0, lo:hi, cols] + _dot(act_ref[lo:hi, :], wout_ref[:, cols])
            return run
        return [piece(0), piece(1)]

    for f in project_items(0):
        f()
    pair_copies(0)
    for c in range(n_sub):
        matmuls = (project_items(c + 1) if c + 1 < n_sub else []) + (output_items(c - 1) if c > 0 else [])
        if c + 1 < n_sub:
            matmuls.pop(0)()
        _interleave(conv_items(c), matmuls)
        if c + 1 < n_sub:
            pair_copies(c + 1)
    for f in output_items(n_sub - 1):
        f()

    hprev_ref[...] = hist_ref[tm * LANE_CHUNKS:total, :]


def _mixer_a_prompt(x, g, w_in, w_dw, b_dw, ln_g, ln_b, w_out):
    b, t, d = x.shape
    tm = TILE_M
    const = lambda shape: pl.BlockSpec(shape, lambda i, j: (0,) * len(shape))
    w_pairs = jnp.tile(w_dw.reshape(CONV_WIDTH, 1, LANE_CHUNKS, LANES), (1, 2, 1, 1))
    w_pairs = w_pairs.reshape(CONV_WIDTH * PAIR_ROWS, LANES).astype(BF16)
    return pl.pallas_call(
        _mixer_a_prompt_kernel,
        grid=(b, t // tm),
        in_specs=[
            pl.BlockSpec((1, tm, d), lambda i, j: (i, j, 0)),
            const((1, d)),
            const((d, 3 * D_CONV)),
            const((CONV_WIDTH * PAIR_ROWS, LANES)),
            const((LANE_CHUNKS, LANES)),
            const((1, D_CONV)),
            const((1, D_CONV)),
            const((D_CONV, d)),
        ],
        out_specs=[
            pl.BlockSpec((1, tm, d), lambda i, j: (i, j, 0)),
            pl.BlockSpec((1, PREFIX_ROWS, D_CONV), lambda i, j: (i, 0, 0)),
        ],
        out_shape=[
            jax.ShapeDtypeStruct((b, t, d), F32),
            jax.ShapeDtypeStruct((b, PREFIX_ROWS, D_CONV), F32),
        ],
        scratch_shapes=[
            pltpu.VMEM(((PREFIX_ROWS + tm) * LANE_CHUNKS, LANES), F32),
            pltpu.VMEM((PREFIX_ROWS * LANE_CHUNKS, LANES), F32),
            pltpu.VMEM(((PREFIX_ROWS + tm) * LANE_CHUNKS, LANES), BF16),
            pltpu.VMEM(((PREFIX_ROWS + tm - 2) * LANE_CHUNKS, LANES), BF16),
            pltpu.VMEM((tm * LANE_CHUNKS, LANES), F32),
            pltpu.VMEM((tm, d), BF16),
            pltpu.VMEM((tm, D_CONV), F32),
            pltpu.VMEM((tm, D_CONV), BF16),
        ],
        compiler_params=pltpu.CompilerParams(
            dimension_semantics=("arbitrary", "arbitrary"), vmem_limit_bytes=VMEM_LIMIT),
        name="mixer_a_prompt",
    )(x, g, w_in, w_pairs, b_dw.reshape(LANE_CHUNKS, LANES), ln_g, ln_b, w_out)


def _mixer_a_sample_kernel(x_ref, st_ref, g_ref, win_ref, wdw_ref, bdw_ref, lng_ref, lnb_ref, wout_ref,
                           y_ref, nst_ref, vf_ref, zg_ref, act_ref):
    nb, ts, d = x_ref.shape
    x = x_ref[...].reshape(nb * ts, d)
    h = (_rms_unit(x) * g_ref[...]).astype(BF16)
    a = _dot(h, win_ref[:, 0:D_CONV])
    gl = _dot(h, win_ref[:, D_CONV:2 * D_CONV])
    v = a * jax.nn.sigmoid(gl)
    zg_ref[...] = _silu(_dot(h, win_ref[:, 2 * D_CONV:3 * D_CONV])).reshape(nb, ts, D_CONV)
    vf_ref[:, 0:CONV_STATE, :] = st_ref[0]
    vf_ref[:, CONV_STATE:CONV_STATE + ts, :] = v.reshape(nb, ts, D_CONV)
    nst_ref[0] = vf_ref[:, ts:ts + CONV_STATE, :]

    bdw = bdw_ref[...]
    lng = lng_ref[...]
    lnb = lnb_ref[...]
    cb = CONV_ROWS // ts

    def chunk(i, carry):
        b0 = pl.multiple_of(i * cb, cb)

        def tap(k):
            return vf_ref[pl.ds(b0, cb), k:k + ts, :].reshape(cb * ts, D_CONV)

        zgate = zg_ref[pl.ds(b0, cb), :, :].reshape(cb * ts, D_CONV)
        gated = _conv_ln_gate(tap, wdw_ref, bdw, lng, lnb, zgate)
        act_ref[pl.ds(b0, cb), :, :] = gated.astype(BF16).reshape(cb, ts, D_CONV)
        return carry

    lax.fori_loop(0, nb // cb, chunk, 0, unroll=2)

    out = _dot(act_ref[...].reshape(nb * ts, D_CONV), wout_ref[...])
    y_ref[...] = (x + out).reshape(nb, ts, d)


def _mixer_a_sample(x, state, g, w_in, w_dw, b_dw, ln_g, ln_b, w_out):
    b, ts, d = x.shape
    nb = SAMPLE_BLOCK
    const = lambda shape: pl.BlockSpec(shape, lambda i: (0,) * len(shape))
    return pl.pallas_call(
        _mixer_a_sample_kernel,
        grid=(b // nb,),
        in_specs=[
            pl.BlockSpec((nb, ts, d), lambda i: (i, 0, 0)),
            pl.BlockSpec((1, nb, CONV_STATE, D_CONV), lambda i: (0, i, 0, 0)),
            const((1, d)),
            const((d, 3 * D_CONV)),
            const((CONV_WIDTH, D_CONV)),
            const((1, D_CONV)),
            const((1, D_CONV)),
            const((1, D_CONV)),
            const((D_CONV, d)),
        ],
        out_specs=[
            pl.BlockSpec((nb, ts, d), lambda i: (i, 0, 0)),
            pl.BlockSpec((1, nb, CONV_STATE, D_CONV), lambda i: (0, i, 0, 0)),
        ],
        out_shape=[
            jax.ShapeDtypeStruct((b, ts, d), F32),
            jax.ShapeDtypeStruct((1, b, CONV_STATE, D_CONV), F32),
        ],
        scratch_shapes=[
            pltpu.VMEM((nb, CONV_STATE + ts + 2, D_CONV), F32),
            pltpu.VMEM((nb, ts, D_CONV), F32),
            pltpu.VMEM((nb, ts, D_CONV), BF16),
        ],
        compiler_params=pltpu.CompilerParams(
            dimension_semantics=("arbitrary",), vmem_limit_bytes=VMEM_LIMIT),
        name="mixer_a_sample",
    )(x, state, g, w_in, w_dw, b_dw, ln_g, ln_b, w_out)


def _project_kvqz(x, gkv_ref, gb_ref, wkv_ref, wq_ref, wz_ref, kg_ref, qg_ref, seg_ref):
    xn = _rms_unit(x)
    hk = (xn * gkv_ref[...]).astype(BF16)
    hq = (xn * gb_ref[...]).astype(BF16)
    seg = seg_ref[...]
    kv = _dot(hk, wkv_ref[...])
    k = _head_rms(kv[:, 0:D_KV], seg, kg_ref[...])
    v = kv[:, D_KV:2 * D_KV]
    q_all = _dot(hq, wq_ref[...])
    q = []
    for g in range(GROUP):
        cols = slice(g * D_KV, (g + 1) * D_KV)
        q.append(_head_rms(q_all[:, cols], seg, qg_ref[...]))
    zg = _silu(_dot(hq, wz_ref[...]))
    return k, v, q, zg


def _lane_segment_mask(kvh):
    lane = lax.broadcasted_iota(jnp.int32, (1, D_KV), 1)
    return (lane >= kvh * HEAD_DIM) & (lane < (kvh + 1) * HEAD_DIM)


def _softmax_unnorm(s, sink):
    m = jnp.maximum(jnp.max(s, axis=-1, keepdims=True), sink)
    p = jnp.exp2(s - m)
    denom = jnp.sum(p, axis=-1, keepdims=True) + jnp.exp2(sink - m)
    return p, denom


def _mixer_b_prompt_kernel(sink_ref, x_ref, gkv_ref, gb_ref, wkv_ref, wq_ref, wz_ref, kg_ref, qg_ref,
                           seg_ref, bias_ref, wout_ref,
                           y_ref, kt_ref, vt_ref,
                           km_ref, vm_ref, kprev_ref, vprev_ref, q_ref, o_ref):
    t = pl.program_id(1)
    tm = x_ref.shape[1]

    zeros = jnp.zeros((N_KV, WINDOW, D_KV), BF16)
    km_ref[:, 0:WINDOW, :] = jnp.where(t > 0, kprev_ref[...], zeros)
    vm_ref[:, 0:WINDOW, :] = jnp.where(t > 0, vprev_ref[...], zeros)

    x = x_ref[0]
    k, v, q, zg = _project_kvqz(x, gkv_ref, gb_ref, wkv_ref, wq_ref, wz_ref, kg_ref, qg_ref, seg_ref)
    for g in range(GROUP):
        q_ref[g] = q[g].astype(BF16)
    for kvh in range(N_KV):
        mask = _lane_segment_mask(kvh)
        km_ref[kvh, WINDOW:WINDOW + tm, :] = jnp.where(mask, k, 0.0).astype(BF16)
        vm_ref[kvh, WINDOW:WINDOW + tm, :] = jnp.where(mask, v, 0.0).astype(BF16)

    @pl.when(t == pl.num_programs(1) - 1)
    def _():
        kt_ref[0] = k[tm - WINDOW:tm, :]
        vt_ref[0] = v[tm - WINDOW:tm, :]

    span = WINDOW + Q_BLOCK

    for j in range(tm // Q_BLOCK):
        r0 = j * Q_BLOCK
        bsel = jnp.where(t == 0, 0, 1) if j == 0 else 1
        qstack = jnp.concatenate([q_ref[g, r0:r0 + Q_BLOCK, :] for g in range(GROUP)], axis=0)
        out = None
        keys = jnp.concatenate([km_ref[kvh, r0:r0 + span, :] for kvh in range(N_KV)], axis=0)
        s_all = _dot_nt(qstack, keys)
        for kvh in range(N_KV):
            s = s_all[:, kvh * span:(kvh + 1) * span] + bias_ref[bsel, kvh]
            probs = []
            for g in range(GROUP):
                p, denom = _softmax_unnorm(s[g * Q_BLOCK:(g + 1) * Q_BLOCK, :], sink_ref[kvh * GROUP + g])
                probs.append((p * (1.0 / denom)).astype(BF16))
            o = _dot(jnp.concatenate(probs, axis=0), vm_ref[kvh, r0:r0 + span, :])
            out = o if out is None else out + o
        for g in range(GROUP):
            o_ref[r0:r0 + Q_BLOCK, g * D_KV:(g + 1) * D_KV] = out[g * Q_BLOCK:(g + 1) * Q_BLOCK, :]

    y_ref[0] = x + _dot((o_ref[...] * zg).astype(BF16), wout_ref[...])

    for kvh in range(N_KV):
        mask = _lane_segment_mask(kvh)
        kprev_ref[kvh] = jnp.where(mask, k[tm - WINDOW:tm, :], 0.0).astype(BF16)
        vprev_ref[kvh] = jnp.where(mask, v[tm - WINDOW:tm, :], 0.0).astype(BF16)


def _mixer_b_prompt(x, sinks, gkv, gb, wkv, wq, wz, kg, qg, seg, bias, wout):
    b, t, d = x.shape
    tm = TILE_M
    const = lambda shape: pl.BlockSpec(shape, lambda i, j: (0,) * len(shape))
    return pl.pallas_call(
        _mixer_b_prompt_kernel,
        grid=(b, t // tm),
        in_specs=[
            pl.BlockSpec(memory_space=pltpu.SMEM),
            pl.BlockSpec((1, tm, d), lambda i, j: (i, j, 0)),
            const((1, d)),
            const((1, d)),
            const((d, 2 * D_KV)),
            const((d, D_ATTN)),
            const((d, D_ATTN)),
            const((1, D_KV)),
            const((1, D_KV)),
            const((D_KV, D_KV)),
            const((2, N_KV, GROUP * Q_BLOCK, WINDOW + Q_BLOCK)),
            const((D_ATTN, d)),
        ],
        out_specs=[
            pl.BlockSpec((1, tm, d), lambda i, j: (i, j, 0)),
            pl.BlockSpec((1, WINDOW, D_KV), lambda i, j: (i, 0, 0)),
            pl.BlockSpec((1, WINDOW, D_KV), lambda i, j: (i, 0, 0)),
        ],
        out_shape=[
            jax.ShapeDtypeStruct((b, t, d), F32),
            jax.ShapeDtypeStruct((b, WINDOW, D_KV), F32),
            jax.ShapeDtypeStruct((b, WINDOW, D_KV), F32),
        ],
        scratch_shapes=[
            pltpu.VMEM((N_KV, WINDOW + tm, D_KV), BF16),
            pltpu.VMEM((N_KV, WINDOW + tm, D_KV), BF16),
            pltpu.VMEM((N_KV, WINDOW, D_KV), BF16),
            pltpu.VMEM((N_KV, WINDOW, D_KV), BF16),
            pltpu.VMEM((GROUP, tm, D_KV), BF16),
            pltpu.VMEM((tm, D_ATTN), F32),
        ],
        compiler_params=pltpu.CompilerParams(
            dimension_semantics=("arbitrary", "arbitrary"), vmem_limit_bytes=VMEM_LIMIT),
        name="mixer_b_prompt",
    )(sinks, x, gkv, gb, wkv, wq, wz, kg, qg, seg, bias, wout)


def _mixer_b_sample_kernel(x_ref, ck_ref, cv_ref, gkv_ref, gb_ref, wkv_ref, wq_ref, wz_ref, kg_ref,
                           qg_ref, seg_ref, bias_ref, sink_ref, wout_ref,
                           y_ref, nk_ref, nv_ref,
                           kc_ref, vc_ref, q_ref, o_ref):
    nb, ts, d = x_ref.shape
    x = x_ref[...].reshape(nb * ts, d)
    k, v, q, zg = _project_kvqz(x, gkv_ref, gb_ref, wkv_ref, wq_ref, wz_ref, kg_ref, qg_ref, seg_ref)
    for g in range(GROUP):
        q_ref[g] = q[g].reshape(nb, ts, D_KV)
    ctx = kc_ref.shape[1]
    kc_ref[:, 0:WINDOW, :] = ck_ref[...]
    vc_ref[:, 0:WINDOW, :] = cv_ref[...]
    kc_ref[:, WINDOW:WINDOW + ts, :] = k.reshape(nb, ts, D_KV)
    vc_ref[:, WINDOW:WINDOW + ts, :] = v.reshape(nb, ts, D_KV)
    kc_ref[:, WINDOW + ts:ctx, :] = jnp.zeros((nb, ctx - WINDOW - ts, D_KV), F32)
    vc_ref[:, WINDOW + ts:ctx, :] = jnp.zeros((nb, ctx - WINDOW - ts, D_KV), F32)
    nk_ref[...] = kc_ref[:, ts:ts + WINDOW, :]
    nv_ref[...] = vc_ref[:, ts:ts + WINDOW, :]

    masks = [_lane_segment_mask(kvh) for kvh in range(N_KV)]
    bias = bias_ref[...]
    sink = sink_ref[...]

    def seq(i, carry):
        kk = kc_ref[i].astype(BF16)
        vv = vc_ref[i].astype(BF16)
        rows = []
        for g in range(GROUP):
            qg_rows = q_ref[g, i]
            for kvh in range(N_KV):
                rows.append(jnp.where(masks[kvh], qg_rows, jnp.zeros_like(qg_rows)))
        lhs = jnp.concatenate(rows, axis=0).astype(BF16)
        s = _dot_nt(lhs, kk) + bias
        p, denom = _softmax_unnorm(s, sink)
        o = _dot((p * (1.0 / denom)).astype(BF16), vv)
        for g in range(GROUP):
            acc = None
            for kvh in range(N_KV):
                r = (g * N_KV + kvh) * ts
                part = jnp.where(masks[kvh], o[r:r + ts, :], 0.0)
                acc = part if acc is None else acc + part
            o_ref[i, :, g * D_KV:(g + 1) * D_KV] = acc
        return carry

    lax.fori_loop(0, nb, seq, 0, unroll=SEQ_UNROLL)

    og = (o_ref[...].reshape(nb * ts, D_ATTN) * zg).astype(BF16)
    y_ref[...] = (x + _dot(og, wout_ref[...])).reshape(nb, ts, d)


def _mixer_b_sample(x, ck, cv, gkv, gb, wkv, wq, wz, kg, qg, seg, bias, sink, wout):
    b, ts, d = x.shape
    nb = SAMPLE_BLOCK_ATTN
    const = lambda shape: pl.BlockSpec(shape, lambda i: (0,) * len(shape))
    rows = N_HEADS * ts
    ctx = bias.shape[1]
    return pl.pallas_call(
        _mixer_b_sample_kernel,
        grid=(b // nb,),
        in_specs=[
            pl.BlockSpec((nb, ts, d), lambda i: (i, 0, 0)),
            pl.BlockSpec((nb, WINDOW, D_KV), lambda i: (i, 0, 0)),
            pl.BlockSpec((nb, WINDOW, D_KV), lambda i: (i, 0, 0)),
            const((1, d)),
            const((1, d)),
            const((d, 2 * D_KV)),
            const((d, D_ATTN)),
            const((d, D_ATTN)),
            const((1, D_KV)),
            const((1, D_KV)),
            const((D_KV, D_KV)),
            const((rows, ctx)),
            const((rows, 1)),
            const((D_ATTN, d)),
        ],
        out_specs=[
            pl.BlockSpec((nb, ts, d), lambda i: (i, 0, 0)),
            pl.BlockSpec((nb, WINDOW, D_KV), lambda i: (i, 0, 0)),
            pl.BlockSpec((nb, WINDOW, D_KV), lambda i: (i, 0, 0)),
        ],
        out_shape=[
            jax.ShapeDtypeStruct((b, ts, d), F32),
            jax.ShapeDtypeStruct((b, WINDOW, D_KV), F32),
            jax.ShapeDtypeStruct((b, WINDOW, D_KV), F32),
        ],
        scratch_shapes=[
            pltpu.VMEM((nb, ctx, D_KV), F32),
            pltpu.VMEM((nb, ctx, D_KV), F32),
            pltpu.VMEM((GROUP, nb, ts, D_KV), F32),
            pltpu.VMEM((nb, ts, D_ATTN), F32),
        ],
        compiler_params=pltpu.CompilerParams(
            dimension_semantics=("arbitrary",), vmem_limit_bytes=VMEM_LIMIT),
        name="mixer_b_sample",
    )(x, ck, cv, gkv, gb, wkv, wq, wz, kg, qg, seg, bias, sink, wout)


def _alibi_slopes():
    return 2.0 ** (-8.0 * np.arange(1, N_HEADS + 1) / N_HEADS)


def _band_bias(n_query, n_key, first_block):
    dist = np.arange(n_query)[:, None] + WINDOW - np.arange(n_key)[None, :]
    allowed = (dist >= 0) & (dist < WINDOW)
    if first_block:
        allowed = allowed & (np.arange(n_key)[None, :] >= WINDOW)
    bias = -_alibi_slopes()[:, None, None] * dist[None].astype(np.float64) * LOG2E
    return np.where(allowed[None], bias, -np.inf).astype(np.float32)


def _group_major(w, axis):
    shape = w.shape
    w = w.reshape(shape[:axis] + (N_KV, GROUP, HEAD_DIM) + shape[axis + 1:])
    w = jnp.swapaxes(w, axis, axis + 1)
    return w.reshape(shape)


def kernel(x_prompt, x_sample, state_conv, cache_k, cache_v, norm_a, w_in_a, w_dw_a, b_dw_a, ln_g_a, ln_b_a, w_out_a, norm_kv, w_kv, k_norm, norm_b, w_in_b, q_norm, sinks_b, w_out_b):
    assert norm_a.shape[0] == 1 and norm_b.shape[0] == 1
    row = lambda v: v.reshape(1, -1).astype(F32)

    a_args = (row(norm_a[0]), w_in_a[0].astype(BF16), w_dw_a[0].astype(F32), row(b_dw_a[0]),
              row(ln_g_a[0]), row(ln_b_a[0]), w_out_a[0].astype(BF16))
    xp, st_p = _mixer_a_prompt(x_prompt, *a_args)
    xs, st_s = _mixer_a_sample(x_sample, state_conv, *a_args)

    wq = _group_major(w_in_b[0][:, :D_ATTN], 1).astype(BF16)
    wz = _group_major(w_in_b[0][:, D_ATTN:], 1).astype(BF16)
    wout = _group_major(w_out_b[0], 0).astype(BF16)
    sinks = sinks_b[0].astype(F32) * LOG2E
    kg = row(jnp.tile(k_norm, N_KV))
    qg = row(jnp.tile(q_norm[0], N_KV)) * (HEAD_DIM ** -0.5 * LOG2E)
    seg = jnp.asarray(np.kron(np.eye(N_KV), np.full((HEAD_DIM, HEAD_DIM), 1.0 / HEAD_DIM)), BF16)
    shared = (row(norm_kv), row(norm_b[0]), w_kv.astype(BF16), wq, wz, kg, qg, seg)

    span = WINDOW + Q_BLOCK
    bias_p = jnp.asarray(np.stack([_band_bias(Q_BLOCK, span, True), _band_bias(Q_BLOCK, span, False)])
                         .reshape(2, N_KV, GROUP * Q_BLOCK, span))
    yp, k_tail, v_tail = _mixer_b_prompt(xp, sinks, *shared, bias_p, wout)

    ts = x_sample.shape[1]
    head_of_row = np.array([kvh * GROUP + g for g in range(GROUP) for kvh in range(N_KV)])
    ctx = -(-(WINDOW + ts) // BF16_ROWS) * BF16_ROWS
    bias_s = jnp.asarray(_band_bias(ts, ctx, False)[head_of_row].reshape(N_HEADS * ts, ctx))
    sink_s = jnp.repeat(sinks[head_of_row], ts).reshape(N_HEADS * ts, 1)
    nb = cache_k.shape[0]
    ys, nk, nv = _mixer_b_sample(xs, cache_k.reshape(nb, WINDOW, D_KV), cache_v.reshape(nb, WINDOW, D_KV),
                                 *shared, bias_s, sink_s, wout)

    kv4 = lambda a: a.reshape(a.shape[0], WINDOW, N_KV, HEAD_DIM)
    return (yp, ys, st_p[None, :, PREFIX_SKIP:, :], st_s, kv4(k_tail), kv4(v_tail), kv4(nk), kv4(nv))
```

```python
import functools

import numpy as np
import jax
import jax.numpy as jnp
from jax import lax
from jax.experimental import pallas as pl
from jax.experimental.pallas import tpu as pltpu

F32 = jnp.float32
BF16 = jnp.bfloat16

D_MODEL = 1024
D_CONV = 1024
CONV_WIDTH = 31
CONV_STATE = CONV_WIDTH - 1
N_HEADS = 16
N_KV = 4
GROUP = N_HEADS // N_KV
HEAD_DIM = 64
D_ATTN = N_HEADS * HEAD_DIM
D_KV = N_KV * HEAD_DIM
WINDOW = 128
Q_BLOCK = 128
RMS_EPS = 1e-6
LN_EPS = 1e-5
LOG2E = float(np.log2(np.e))

SUBLANES = 8
LANES = 128
LANE_CHUNKS = D_CONV // LANES
assert LANE_CHUNKS == SUBLANES
BF16_ROWS = 16
PAIR_ROWS = 2 * LANE_CHUNKS
assert PAIR_ROWS == BF16_ROWS
PREFIX_ROWS = 32
PREFIX_SKIP = PREFIX_ROWS - CONV_STATE
TILE_M = 512
SUB_M = 512
CONV_ROWS = 32
SAMPLE_BLOCK = 32
SAMPLE_BLOCK_ATTN = 16
SEQ_UNROLL = 4
VMEM_LIMIT = 56 * 1024 * 1024


def _dot(a, b):
    return jnp.dot(a, b, preferred_element_type=F32)


def _dot_nt(a, b):
    return lax.dot_general(a, b, (((1,), (1,)), ((), ())), preferred_element_type=F32)


def _rms_unit(x):
    return x * lax.rsqrt(jnp.mean(x * x, axis=-1, keepdims=True) + RMS_EPS)


def _silu(x):
    return x * jax.nn.sigmoid(x)


def _head_rms(x, seg_mean, gain):
    ms = _dot((x * x).astype(BF16), seg_mean)
    return x * lax.rsqrt(ms + RMS_EPS) * gain


def _conv_ln_gate(tap, wdw_ref, bdw, lng, lnb, zgate):
    acc = tap(0) * wdw_ref[0:1, :] + bdw
    for k in range(1, CONV_WIDTH):
        acc = acc + tap(k) * wdw_ref[k:k + 1, :]
    mu = jnp.mean(acc, axis=-1, keepdims=True)
    xc = acc - mu
    var = jnp.mean(xc * xc, axis=-1, keepdims=True)
    y = xc * lax.rsqrt(var + LN_EPS) * lng + lnb
    return _silu(y) * zgate


def _interleave(first, second):
    if not first or not second:
        for f in first + second:
            f()
        return
    long, short = (first, second) if len(first) >= len(second) else (second, first)
    done = 0
    for i, f in enumerate(long):
        f()
        want = (i + 1) * len(short) // len(long)
        while done < want:
            short[done]()
            done += 1


def _mixer_a_prompt_kernel(x_ref, g_ref, win_ref, w16_ref, bdw_ref, lng_ref, lnb_ref, wout_ref,
                           y_ref, st_ref, hist_ref, hprev_ref, even_ref, odd_ref, tmp_ref, h_ref, zg_ref,
                           act_ref):
    t = pl.program_id(1)
    tm = x_ref.shape[1]
    prefix = PREFIX_ROWS * LANE_CHUNKS
    total = prefix + tm * LANE_CHUNKS

    hist_ref[0:prefix, :] = jnp.where(t > 0, hprev_ref[...], jnp.zeros((prefix, LANES), F32))

    lng = lng_ref[...]
    lnb = lnb_ref[...]
    rows = CONV_ROWS * LANE_CHUNKS
    n_sub = tm // SUB_M
    half = D_CONV // 2

    def project_items(c):
        lo, hi = c * SUB_M, (c + 1) * SUB_M

        def norm():
            h_ref[lo:hi, :] = (_rms_unit(x_ref[0, lo:hi, :]) * g_ref[...]).astype(BF16)

        def glu(n):
            def run():
                h = h_ref[lo:hi, :]
                a = _dot(h, win_ref[:, n * half:(n + 1) * half])
                gl = _dot(h, win_ref[:, D_CONV + n * half:D_CONV + (n + 1) * half])
                v = a * jax.nn.sigmoid(gl)
                for ch in range(half // LANES):
                    dst = prefix + lo * LANE_CHUNKS + n * (half // LANES) + ch
                    hist_ref[pl.ds(dst, SUB_M, stride=LANE_CHUNKS), :] = v[:, ch * LANES:(ch + 1) * LANES]
                if c == n_sub - 1:
                    st_ref[0, :, n * half:(n + 1) * half] = v[SUB_M - PREFIX_ROWS:SUB_M, :]
            return run

        def gate(n):
            def run():
                z = _dot(h_ref[lo:hi, :], win_ref[:, 2 * D_CONV + n * half:2 * D_CONV + (n + 1) * half])
                zg_ref[lo:hi, n * half:(n + 1) * half] = _silu(z)
            return run

        return [norm, glu(0), gate(0), glu(1), gate(1)]

    def pair_copies(c):
        lo, hi = c * SUB_M, (c + 1) * SUB_M
        e_lo = 0 if c == 0 else prefix + lo * LANE_CHUNKS
        e_hi = prefix + hi * LANE_CHUNKS
        even_ref[e_lo:e_hi, :] = hist_ref[e_lo:e_hi, :].astype(BF16)
        o_lo = 0 if c == 0 else e_lo - PAIR_ROWS
        o_hi = e_hi - PAIR_ROWS
        odd_ref[o_lo:o_hi, :] = hist_ref[o_lo + LANE_CHUNKS:o_hi + LANE_CHUNKS, :].astype(BF16)

    def conv_items(c):
        def chunk(i):
            def run():
                r0 = i * CONV_ROWS
                acc = None
                for k in range(CONV_WIDTH):
                    tau = r0 + PREFIX_SKIP + k
                    src, row = ((even_ref, tau * LANE_CHUNKS) if tau % 2 == 0
                                else (odd_ref, (tau - 1) * LANE_CHUNKS))
                    tap = src[row:row + rows, :].astype(F32).reshape(CONV_ROWS // 2, PAIR_ROWS, LANES)
                    term = tap * w16_ref[k * PAIR_ROWS:(k + 1) * PAIR_ROWS, :].astype(F32)
                    acc = term if acc is None else acc + term
                acc = acc.reshape(CONV_ROWS, LANE_CHUNKS, LANES) + bdw_ref[...]
                tmp_ref[i * rows:(i + 1) * rows, :] = acc.reshape(rows, LANES)
                conv = jnp.concatenate(
                    [tmp_ref[pl.ds(i * rows + ch, CONV_ROWS, stride=LANE_CHUNKS), :] for ch in range(LANE_CHUNKS)],
                    axis=1)
                mu = jnp.mean(conv, axis=-1, keepdims=True)
                xc = conv - mu
                var = jnp.mean(xc * xc, axis=-1, keepdims=True)
                y = xc * lax.rsqrt(var + LN_EPS) * lng + lnb
                act_ref[r0:r0 + CONV_ROWS, :] = (_silu(y) * zg_ref[r0:r0 + CONV_ROWS, :]).astype(BF16)
            return run
        return [chunk(i) for i in range(c * SUB_M // CONV_ROWS, (c + 1) * SUB_M // CONV_ROWS)]

    def output_items(c):
        lo, hi = c * SUB_M, (c + 1) * SUB_M

        def piece(n):
            def run():
                cols = slice(n * half, (n + 1) * half)
                y_ref[0, lo:hi, cols] = x_ref[0, lo:hi, cols] + _dot(act_ref[lo:hi, :], wout_ref[:, cols])
            return run
        return [piece(0), piece(1)]

    for f in project_items(0):
        f()
    pair_copies(0)
    for c in range(n_sub):
        matmuls = (project_items(c + 1) if c + 1 < n_sub else []) + (output_items(c - 1) if c > 0 else [])
        if c + 1 < n_sub:
            matmuls.pop(0)()
        _interleave(conv_items(c), matmuls)
        if c + 1 < n_sub:
            pair_copies(c + 1)
    for f in output_items(n_sub - 1):
        f()

    hprev_ref[...] = hist_ref[tm * LANE_CHUNKS:total, :]


def _mixer_a_prompt(x, g, w_in, w_dw, b_dw, ln_g, ln_b, w_out):
    b, t, d = x.shape
    tm = TILE_M
    const = lambda shape: pl.BlockSpec(shape, lambda i, j: (0,) * len(shape))
    w_pairs = jnp.tile(w_dw.reshape(CONV_WIDTH, 1, LANE_CHUNKS, LANES), (1, 2, 1, 1))
    w_pairs = w_pairs.reshape(CONV_WIDTH * PAIR_ROWS, LANES).astype(BF16)
    return pl.pallas_call(
        _mixer_a_prompt_kernel,
        grid=(b, t // tm),
        in_specs=[
            pl.BlockSpec((1, tm, d), lambda i, j: (i, j, 0)),
            const((1, d)),
            const((d, 3 * D_CONV)),
            const((CONV_WIDTH * PAIR_ROWS, LANES)),
            const((LANE_CHUNKS, LANES)),
            const((1, D_CONV)),
            const((1, D_CONV)),
            const((D_CONV, d)),
        ],
        out_specs=[
            pl.BlockSpec((1, tm, d), lambda i, j: (i, j, 0)),
            pl.BlockSpec((1, PREFIX_ROWS, D_CONV), lambda i, j: (i, 0, 0)),
        ],
        out_shape=[
            jax.ShapeDtypeStruct((b, t, d), F32),
            jax.ShapeDtypeStruct((b, PREFIX_ROWS, D_CONV), F32),
        ],
        scratch_shapes=[
            pltpu.VMEM(((PREFIX_ROWS + tm) * LANE_CHUNKS, LANES), F32),
            pltpu.VMEM((PREFIX_ROWS * LANE_CHUNKS, LANES), F32),
            pltpu.VMEM(((PREFIX_ROWS + tm) * LANE_CHUNKS, LANES), BF16),
            pltpu.VMEM(((PREFIX_ROWS + tm - 2) * LANE_CHUNKS, LANES), BF16),
            pltpu.VMEM((tm * LANE_CHUNKS, LANES), F32),
            pltpu.VMEM((tm, d), BF16),
            pltpu.VMEM((tm, D_CONV), F32),
            pltpu.VMEM((tm, D_CONV), BF16),
        ],
        compiler_params=pltpu.CompilerParams(
            dimension_semantics=("arbitrary", "arbitrary"), vmem_limit_bytes=VMEM_LIMIT),
        name="mixer_a_prompt",
    )(x, g, w_in, w_pairs, b_dw.reshape(LANE_CHUNKS, LANES), ln_g, ln_b, w_out)


def _mixer_a_sample_kernel(x_ref, st_ref, g_ref, win_ref, wdw_ref, bdw_ref, lng_ref, lnb_ref, wout_ref,
                           y_ref, nst_ref, vf_ref, zg_ref, act_ref):
    nb, ts, d = x_ref.shape
    x = x_ref[...].reshape(nb * ts, d)
    h = (_rms_unit(x) * g_ref[...]).astype(BF16)
    a = _dot(h, win_ref[:, 0:D_CONV])
    gl = _dot(h, win_ref[:, D_CONV:2 * D_CONV])
    v = a * jax.nn.sigmoid(gl)
    zg_ref[...] = _silu(_dot(h, win_ref[:, 2 * D_CONV:3 * D_CONV])).reshape(nb, ts, D_CONV)
    vf_ref[:, 0:CONV_STATE, :] = st_ref[0]
    vf_ref[:, CONV_STATE:CONV_STATE + ts, :] = v.reshape(nb, ts, D_CONV)
    nst_ref[0] = vf_ref[:, ts:ts + CONV_STATE, :]

    bdw = bdw_ref[...]
    lng = lng_ref[...]
    lnb = lnb_ref[...]
    cb = CONV_ROWS // ts

    def chunk(i, carry):
        b0 = pl.multiple_of(i * cb, cb)

        def tap(k):
            return vf_ref[pl.ds(b0, cb), k:k + ts, :].reshape(cb * ts, D_CONV)

        zgate = zg_ref[pl.ds(b0, cb), :, :].reshape(cb * ts, D_CONV)
        gated = _conv_ln_gate(tap, wdw_ref, bdw, lng, lnb, zgate)
        act_ref[pl.ds(b0, cb), :, :] = gated.astype(BF16).reshape(cb, ts, D_CONV)
        return carry

    lax.fori_loop(0, nb // cb, chunk, 0, unroll=2)

    out = _dot(act_ref[...].reshape(nb * ts, D_CONV), wout_ref[...])
    y_ref[...] = (x + out).reshape(nb, ts, d)


def _mixer_a_sample(x, state, g, w_in, w_dw, b_dw, ln_g, ln_b, w_out):
    b, ts, d = x.shape
    nb = SAMPLE_BLOCK
    const = lambda shape: pl.BlockSpec(shape, lambda i: (0,) * len(shape))
    return pl.pallas_call(
        _mixer_a_sample_kernel,
        grid=(b // nb,),
        in_specs=[
            pl.BlockSpec((nb, ts, d), lambda i: (i, 0, 0)),
            pl.BlockSpec((1, nb, CONV_STATE, D_CONV), lambda i: (0, i, 0, 0)),
            const((1, d)),
            const((d, 3 * D_CONV)),
            const((CONV_WIDTH, D_CONV)),
            const((1, D_CONV)),
            const((1, D_CONV)),
            const((1, D_CONV)),
            const((D_CONV, d)),
        ],
        out_specs=[
            pl.BlockSpec((nb, ts, d), lambda i: (i, 0, 0)),
            pl.BlockSpec((1, nb, CONV_STATE, D_CONV), lambda i: (0, i, 0, 0)),
        ],
        out_shape=[
            jax.ShapeDtypeStruct((b, ts, d), F32),
            jax.ShapeDtypeStruct((1, b, CONV_STATE, D_CONV), F32),
        ],
        scratch_shapes=[
            pltpu.VMEM((nb, CONV_STATE + ts + 2, D_CONV), F32),
            pltpu.VMEM((nb, ts, D_CONV), F32),
            pltpu.VMEM((nb, ts, D_CONV), BF16),
        ],
        compiler_params=pltpu.CompilerParams(
            dimension_semantics=("arbitrary",), vmem_limit_bytes=VMEM_LIMIT),
        name="mixer_a_sample",
    )(x, state, g, w_in, w_dw, b_dw, ln_g, ln_b, w_out)


def _project_kvqz(x, gkv_ref, gb_ref, wkv_ref, wq_ref, wz_ref, kg_ref, qg_ref, seg_ref):
    xn = _rms_unit(x)
    hk = (xn * gkv_ref[...]).astype(BF16)
    hq = (xn * gb_ref[...]).astype(BF16)
    seg = seg_ref[...]
    kv = _dot(hk, wkv_ref[...])
    k = _head_rms(kv[:, 0:D_KV], seg, kg_ref[...])
    v = kv[:, D_KV:2 * D_KV]
    q_all = _dot(hq, wq_ref[...])
    q = []
    for g in range(GROUP):
        cols = slice(g * D_KV, (g + 1) * D_KV)
        q.append(_head_rms(q_all[:, cols], seg, qg_ref[...]))
    zg = _silu(_dot(hq, wz_ref[...]))
    return k, v, q, zg


def _lane_segment_mask(kvh):
    lane = lax.broadcasted_iota(jnp.int32, (1, D_KV), 1)
    return (lane >= kvh * HEAD_DIM) & (lane < (kvh + 1) * HEAD_DIM)


def _softmax_unnorm(s, sink):
    m = jnp.maximum(jnp.max(s, axis=-1, keepdims=True), sink)
    p = jnp.exp2(s - m)
    denom = jnp.sum(p, axis=-1, keepdims=True) + jnp.exp2(sink - m)
    return p, denom


def _mixer_b_prompt_kernel(sink_ref, x_ref, gkv_ref, gb_ref, wkv_ref, wq_ref, wz_ref, kg_ref, qg_ref,
                           seg_ref, bias_ref, wout_ref,
                           y_ref, kt_ref, vt_ref,
                           km_ref, vm_ref, kprev_ref, vprev_ref, q_ref, o_ref, hq_ref, zg_ref):
    t = pl.program_id(1)
    tm = x_ref.shape[1]

    zeros = jnp.zeros((N_KV, WINDOW, D_KV), BF16)
    km_ref[:, 0:WINDOW, :] = jnp.where(t > 0, kprev_ref[...], zeros)
    vm_ref[:, 0:WINDOW, :] = jnp.where(t > 0, vprev_ref[...], zeros)

    x = x_ref[0]
    xn = _rms_unit(x)
    hk = (xn * gkv_ref[...]).astype(BF16)
    hq_ref[...] = (xn * gb_ref[...]).astype(BF16)
    seg = seg_ref[...]
    kv = _dot(hk, wkv_ref[...])
    k = _head_rms(kv[:, 0:D_KV], seg, kg_ref[...])
    v = kv[:, D_KV:2 * D_KV]
    for kvh in range(N_KV):
        mask = _lane_segment_mask(kvh)
        km_ref[kvh, WINDOW:WINDOW + tm, :] = jnp.where(mask, k, 0.0).astype(BF16)
        vm_ref[kvh, WINDOW:WINDOW + tm, :] = jnp.where(mask, v, 0.0).astype(BF16)

    @pl.when(t == pl.num_programs(1) - 1)
    def _():
        kt_ref[0] = k[tm - WINDOW:tm, :]
        vt_ref[0] = v[tm - WINDOW:tm, :]

    span = WINDOW + Q_BLOCK

    def project_block(j):
        r0 = j * Q_BLOCK
        hq = hq_ref[r0:r0 + Q_BLOCK, :]
        q_all = _dot(hq, wq_ref[...])
        for g in range(GROUP):
            cols = slice(g * D_KV, (g + 1) * D_KV)
            q_ref[g, r0:r0 + Q_BLOCK, :] = _head_rms(q_all[:, cols], seg, qg_ref[...]).astype(BF16)
        zg_ref[r0:r0 + Q_BLOCK, :] = _silu(_dot(hq, wz_ref[...]))

    def attend_block(j):
        r0 = j * Q_BLOCK
        bsel = jnp.where(t == 0, 0, 1) if j == 0 else 1
        qstack = jnp.concatenate([q_ref[g, r0:r0 + Q_BLOCK, :] for g in range(GROUP)], axis=0)
        out = None
        keys = jnp.concatenate([km_ref[kvh, r0:r0 + span, :] for kvh in range(N_KV)], axis=0)
        s_all = _dot_nt(qstack, keys)
        for kvh in range(N_KV):
            s = s_all[:, kvh * span:(kvh + 1) * span] + bias_ref[bsel, kvh]
            probs = []
            for g in range(GROUP):
                p, denom = _softmax_unnorm(s[g * Q_BLOCK:(g + 1) * Q_BLOCK, :], sink_ref[kvh * GROUP + g])
                probs.append((p * (1.0 / denom)).astype(BF16))
            o = _dot(jnp.concatenate(probs, axis=0), vm_ref[kvh, r0:r0 + span, :])
            out = o if out is None else out + o
        for g in range(GROUP):
            o_ref[r0:r0 + Q_BLOCK, g * D_KV:(g + 1) * D_KV] = out[g * Q_BLOCK:(g + 1) * Q_BLOCK, :]

    def output_block(j):
        rows = slice(j * Q_BLOCK, (j + 1) * Q_BLOCK)
        gated = (o_ref[rows, :] * zg_ref[rows, :]).astype(BF16)
        y_ref[0, rows, :] = x_ref[0, rows, :] + _dot(gated, wout_ref[...])

    n_blocks = tm // Q_BLOCK
    project_block(0)
    for j in range(n_blocks):
        if j + 1 < n_blocks:
            project_block(j + 1)
        attend_block(j)
        if j > 0:
            output_block(j - 1)
    output_block(n_blocks - 1)

    for kvh in range(N_KV):
        mask = _lane_segment_mask(kvh)
        kprev_ref[kvh] = jnp.where(mask, k[tm - WINDOW:tm, :], 0.0).astype(BF16)
        vprev_ref[kvh] = jnp.where(mask, v[tm - WINDOW:tm, :], 0.0).astype(BF16)


def _mixer_b_prompt(x, sinks, gkv, gb, wkv, wq, wz, kg, qg, seg, bias, wout):
    b, t, d = x.shape
    tm = TILE_M
    const = lambda shape: pl.BlockSpec(shape, lambda i, j: (0,) * len(shape))
    return pl.pallas_call(
        _mixer_b_prompt_kernel,
        grid=(b, t // tm),
        in_specs=[
            pl.BlockSpec(memory_space=pltpu.SMEM),
            pl.BlockSpec((1, tm, d), lambda i, j: (i, j, 0)),
            const((1, d)),
            const((1, d)),
            const((d, 2 * D_KV)),
            const((d, D_ATTN)),
            const((d, D_ATTN)),
            const((1, D_KV)),
            const((1, D_KV)),
            const((D_KV, D_KV)),
            const((2, N_KV, GROUP * Q_BLOCK, WINDOW + Q_BLOCK)),
            const((D_ATTN, d)),
        ],
        out_specs=[
            pl.BlockSpec((1, tm, d), lambda i, j: (i, j, 0)),
            pl.BlockSpec((1, WINDOW, D_KV), lambda i, j: (i, 0, 0)),
            pl.BlockSpec((1, WINDOW, D_KV), lambda i, j: (i, 0, 0)),
        ],
        out_shape=[
            jax.ShapeDtypeStruct((b, t, d), F32),
            jax.ShapeDtypeStruct((b, WINDOW, D_KV), F32),
            jax.ShapeDtypeStruct((b, WINDOW, D_KV), F32),
        ],
        scratch_shapes=[
            pltpu.VMEM((N_KV, WINDOW + tm, D_KV), BF16),
            pltpu.VMEM((N_KV, WINDOW + tm, D_KV), BF16),
            pltpu.VMEM((N_KV, WINDOW, D_KV), BF16),
            pltpu.VMEM((N_KV, WINDOW, D_KV), BF16),
            pltpu.VMEM((GROUP, tm, D_KV), BF16),
            pltpu.VMEM((tm, D_ATTN), F32),
            pltpu.VMEM((tm, d), BF16),
            pltpu.VMEM((tm, D_ATTN), F32),
        ],
        compiler_params=pltpu.CompilerParams(
            dimension_semantics=("arbitrary", "arbitrary"), vmem_limit_bytes=VMEM_LIMIT),
        name="mixer_b_prompt",
    )(sinks, x, gkv, gb, wkv, wq, wz, kg, qg, seg, bias, wout)


def _mixer_b_sample_kernel(x_ref, ck_ref, cv_ref, gkv_ref, gb_ref, wkv_ref, wq_ref, wz_ref, kg_ref,
                           qg_ref, seg_ref, bias_ref, sink_ref, wout_ref,
                           y_ref, nk_ref, nv_ref,
                           kc_ref, vc_ref, q_ref, o_ref):
    nb, ts, d = x_ref.shape
    x = x_ref[...].reshape(nb * ts, d)
    k, v, q, zg = _project_kvqz(x, gkv_ref, gb_ref, wkv_ref, wq_ref, wz_ref, kg_ref, qg_ref, seg_ref)
    for g in range(GROUP):
        q_ref[g] = q[g].reshape(nb, ts, D_KV)
    ctx = kc_ref.shape[1]
    kc_ref[:, 0:WINDOW, :] = ck_ref[...]
    vc_ref[:, 0:WINDOW, :] = cv_ref[...]
    kc_ref[:, WINDOW:WINDOW + ts, :] = k.reshape(nb, ts, D_KV)
    vc_ref[:, WINDOW:WINDOW + ts, :] = v.reshape(nb, ts, D_KV)
    kc_ref[:, WINDOW + ts:ctx, :] = jnp.zeros((nb, ctx - WINDOW - ts, D_KV), F32)
    vc_ref[:, WINDOW + ts:ctx, :] = jnp.zeros((nb, ctx - WINDOW - ts, D_KV), F32)
    nk_ref[...] = kc_ref[:, ts:ts + WINDOW, :]
    nv_ref[...] = vc_ref[:, ts:ts + WINDOW, :]

    masks = [_lane_segment_mask(kvh) for kvh in range(N_KV)]
    bias = bias_ref[...]
    sink = sink_ref[...]

    def seq(i, carry):
        kk = kc_ref[i].astype(BF16)
        vv = vc_ref[i].astype(BF16)
        rows = []
        for g in range(GROUP):
            qg_rows = q_ref[g, i]
            for kvh in range(N_KV):
                rows.append(jnp.where(masks[kvh], qg_rows, jnp.zeros_like(qg_rows)))
        lhs = jnp.concatenate(rows, axis=0).astype(BF16)
        s = _dot_nt(lhs, kk) + bias
        p, denom = _softmax_unnorm(s, sink)
        o = _dot((p * (1.0 / denom)).astype(BF16), vv)
        for g in range(GROUP):
            acc = None
            for kvh in range(N_KV):
                r = (g * N_KV + kvh) * ts
                part = jnp.where(masks[kvh], o[r:r + ts, :], 0.0)
                acc = part if acc is None else acc + part
            o_ref[i, :, g * D_KV:(g + 1) * D_KV] = acc
        return carry

    lax.fori_loop(0, nb, seq, 0, unroll=SEQ_UNROLL)

    og = (o_ref[...].reshape(nb * ts, D_ATTN) * zg).astype(BF16)
    y_ref[...] = (x + _dot(og, wout_ref[...])).reshape(nb, ts, d)


def _mixer_b_sample(x, ck, cv, gkv, gb, wkv, wq, wz, kg, qg, seg, bias, sink, wout):
    b, ts, d = x.shape
    nb = SAMPLE_BLOCK_ATTN
    const = lambda shape: pl.BlockSpec(shape, lambda i: (0,) * len(shape))
    rows = N_HEADS * ts
    ctx = bias.shape[1]
    return pl.pallas_call(
        _mixer_b_sample_kernel,
        grid=(b // nb,),
        in_specs=[
            pl.BlockSpec((nb, ts, d), lambda i: (i, 0, 0)),
            pl.BlockSpec((nb, WINDOW, D_KV), lambda i: (i, 0, 0)),
            pl.BlockSpec((nb, WINDOW, D_KV), lambda i: (i, 0, 0)),
            const((1, d)),
            const((1, d)),
            const((d, 2 * D_KV)),
            const((d, D_ATTN)),
            const((d, D_ATTN)),
            const((1, D_KV)),
            const((1, D_KV)),
            const((D_KV, D_KV)),
            const((rows, ctx)),
            const((rows, 1)),
            const((D_ATTN, d)),
        ],
        out_specs=[
            pl.BlockSpec((nb, ts, d), lambda i: (i, 0, 0)),
            pl.BlockSpec((nb, WINDOW, D_KV), lambda i: (i, 0, 0)),
            pl.BlockSpec((nb, WINDOW, D_KV), lambda i: (i, 0, 0)),
        ],
        out_shape=[
            jax.ShapeDtypeStruct((b, ts, d), F32),
            jax.ShapeDtypeStruct((b, WINDOW, D_KV), F32),
            jax.ShapeDtypeStruct((b, WINDOW, D_KV), F32),
        ],
        scratch_shapes=[
            pltpu.VMEM((nb, ctx, D_KV), F32),
            pltpu.VMEM((nb, ctx, D_KV), F32),
            pltpu.VMEM((GROUP, nb, ts, D_KV), F32),
            pltpu.VMEM((nb, ts, D_ATTN), F32),
        ],
        compiler_params=pltpu.CompilerParams(
            dimension_semantics=("arbitrary",), vmem_limit_bytes=VMEM_LIMIT),
        name="mixer_b_sample",
    )(x, ck, cv, gkv, gb, wkv, wq, wz, kg, qg, seg, bias, sink, wout)


def _alibi_slopes():
    return 2.0 ** (-8.0 * np.arange(1, N_HEADS + 1) / N_HEADS)


def _band_bias(n_query, n_key, first_block):
    dist = np.arange(n_query)[:, None] + WINDOW - np.arange(n_key)[None, :]
    allowed = (dist >= 0) & (dist < WINDOW)
    if first_block:
        allowed = allowed & (np.arange(n_key)[None, :] >= WINDOW)
    bias = -_alibi_slopes()[:, None, None] * dist[None].astype(np.float64) * LOG2E
    return np.where(allowed[None], bias, -np.inf).astype(np.float32)


def _group_major(w, axis):
    shape = w.shape
    w = w.reshape(shape[:axis] + (N_KV, GROUP, HEAD_DIM) + shape[axis + 1:])
    w = jnp.swapaxes(w, axis, axis + 1)
    return w.reshape(shape)


def kernel(x_prompt, x_sample, state_conv, cache_k, cache_v, norm_a, w_in_a, w_dw_a, b_dw_a, ln_g_a, ln_b_a, w_out_a, norm_kv, w_kv, k_norm, norm_b, w_in_b, q_norm, sinks_b, w_out_b):
    assert norm_a.shape[0] == 1 and norm_b.shape[0] == 1
    row = lambda v: v.reshape(1, -1).astype(F32)

    a_args = (row(norm_a[0]), w_in_a[0].astype(BF16), w_dw_a[0].astype(F32), row(b_dw_a[0]),
              row(ln_g_a[0]), row(ln_b_a[0]), w_out_a[0].astype(BF16))
    xp, st_p = _mixer_a_prompt(x_prompt, *a_args)
    xs, st_s = _mixer_a_sample(x_sample, state_conv, *a_args)

    wq = _group_major(w_in_b[0][:, :D_ATTN], 1).astype(BF16)
    wz = _group_major(w_in_b[0][:, D_ATTN:], 1).astype(BF16)
    wout = _group_major(w_out_b[0], 0).astype(BF16)
    sinks = sinks_b[0].astype(F32) * LOG2E
    kg = row(jnp.tile(k_norm, N_KV))
    qg = row(jnp.tile(q_norm[0], N_KV)) * (HEAD_DIM ** -0.5 * LOG2E)
    seg = jnp.asarray(np.kron(np.eye(N_KV), np.full((HEAD_DIM, HEAD_DIM), 1.0 / HEAD_DIM)), BF16)
    shared = (row(norm_kv), row(norm_b[0]), w_kv.astype(BF16), wq, wz, kg, qg, seg)

    span = WINDOW + Q_BLOCK
    bias_p = jnp.asarray(np.stack([_band_bias(Q_BLOCK, span, True), _band_bias(Q_BLOCK, span, False)])
                         .reshape(2, N_KV, GROUP * Q_BLOCK, span))
    yp, k_tail, v_tail = _mixer_b_prompt(xp, sinks, *shared, bias_p, wout)

    ts = x_sample.shape[1]
    head_of_row = np.array([kvh * GROUP + g for g in range(GROUP) for kvh in range(N_KV)])
    ctx = -(-(WINDOW + ts) // BF16_ROWS) * BF16_ROWS
    bias_s = jnp.asarray(_band_bias(ts, ctx, False)[head_of_row].reshape(N_HEADS * ts, ctx))
    sink_s = jnp.repeat(sinks[head_of_row], ts).reshape(N_HEADS * ts, 1)
    nb = cache_k.shape[0]
    ys, nk, nv = _mixer_b_sample(xs, cache_k.reshape(nb, WINDOW, D_KV), cache_v.reshape(nb, WINDOW, D_KV),
                                 *shared, bias_s, sink_s, wout)

    kv4 = lambda a: a.reshape(a.shape[0], WINDOW, N_KV, HEAD_DIM)
    return (yp, ys, st_p[None, :, PREFIX_SKIP:, :], st_s, kv4(k_tail), kv4(v_tail), kv4(nk), kv4(nv))
```

```python
import numpy as np
import jax
import jax.numpy as jnp
from jax import lax
from jax.experimental import pallas as pl
from jax.experimental.pallas import tpu as pltpu

F32 = jnp.float32
BF16 = jnp.bfloat16

D_MODEL = 1024
D_CONV = 1024
CONV_WIDTH = 31
CONV_STATE = CONV_WIDTH - 1
N_HEADS = 16
N_KV = 4
GROUP = N_HEADS // N_KV
HEAD_DIM = 64
D_ATTN = N_HEADS * HEAD_DIM
D_KV = N_KV * HEAD_DIM
WINDOW = 128
Q_BLOCK = 128
RMS_EPS = 1e-6
LN_EPS = 1e-5
LOG2E = float(np.log2(np.e))

SUBLANES = 8
LANES = 128
LANE_CHUNKS = D_CONV // LANES
assert LANE_CHUNKS == SUBLANES
BF16_ROWS = 16
PAIR_ROWS = 2 * LANE_CHUNKS
assert PAIR_ROWS == BF16_ROWS
PREFIX_ROWS = 32
PREFIX_SKIP = PREFIX_ROWS - CONV_STATE
TILE_M = 1024
CONV_ROWS = 32
SAMPLE_BLOCK = 32
SAMPLE_BLOCK_ATTN = 16
SEQ_UNROLL = 4
VMEM_LIMIT = 56 * 1024 * 1024


def _dot(a, b):
    return jnp.dot(a, b, preferred_element_type=F32)


def _dot_nt(a, b):
    return lax.dot_general(a, b, (((1,), (1,)), ((), ())), preferred_element_type=F32)


def _rms_unit(x):
    return x * lax.rsqrt(jnp.mean(x * x, axis=-1, keepdims=True) + RMS_EPS)


def _silu(x):
    return x * jax.nn.sigmoid(x)


def _head_rms(x, seg_mean, gain):
    ms = _dot((x * x).astype(BF16), seg_mean)
    return x * lax.rsqrt(ms + RMS_EPS) * gain


def _conv_ln_gate(tap, wdw_ref, bdw, lng, lnb, zgate):
    acc = tap(0) * wdw_ref[0:1, :] + bdw
    for k in range(1, CONV_WIDTH):
        acc = acc + tap(k) * wdw_ref[k:k + 1, :]
    mu = jnp.mean(acc, axis=-1, keepdims=True)
    xc = acc - mu
    var = jnp.mean(xc * xc, axis=-1, keepdims=True)
    y = xc * lax.rsqrt(var + LN_EPS) * lng + lnb
    return _silu(y) * zgate


def _mixer_a_prompt_kernel(x_ref, g_ref, win_ref, w16_ref, bdw_ref, lng_ref, lnb_ref, wout_ref,
                           y_ref, st_ref, hist_ref, hprev_ref, even_ref, odd_ref, tmp_ref, zg_ref, act_ref):
    t = pl.program_id(1)
    tm = x_ref.shape[1]
    prefix = PREFIX_ROWS * LANE_CHUNKS
    total = prefix + tm * LANE_CHUNKS

    hist_ref[0:prefix, :] = jnp.where(t > 0, hprev_ref[...], jnp.zeros((prefix, LANES), F32))

    x = x_ref[0]
    h = (_rms_unit(x) * g_ref[...]).astype(BF16)
    a = _dot(h, win_ref[:, 0:D_CONV])
    gl = _dot(h, win_ref[:, D_CONV:2 * D_CONV])
    v = a * jax.nn.sigmoid(gl)
    for c in range(LANE_CHUNKS):
        hist_ref[pl.ds(prefix + c, tm, stride=LANE_CHUNKS), :] = v[:, c * LANES:(c + 1) * LANES]
    zg_ref[...] = _silu(_dot(h, win_ref[:, 2 * D_CONV:3 * D_CONV]))

    @pl.when(t == pl.num_programs(1) - 1)
    def _():
        st_ref[0] = v[tm - PREFIX_ROWS:tm, :]

    even_ref[...] = hist_ref[...].astype(BF16)
    odd_ref[...] = hist_ref[LANE_CHUNKS:total - LANE_CHUNKS, :].astype(BF16)

    lng = lng_ref[...]
    lnb = lnb_ref[...]
    rows = CONV_ROWS * LANE_CHUNKS
    for i in range(tm // CONV_ROWS):
        r0 = i * CONV_ROWS
        acc = None
        for k in range(CONV_WIDTH):
            tau = r0 + PREFIX_SKIP + k
            src, row = (even_ref, tau * LANE_CHUNKS) if tau % 2 == 0 else (odd_ref, (tau - 1) * LANE_CHUNKS)
            tap = src[row:row + rows, :].astype(F32).reshape(CONV_ROWS // 2, PAIR_ROWS, LANES)
            term = tap * w16_ref[k * PAIR_ROWS:(k + 1) * PAIR_ROWS, :].astype(F32)
            acc = term if acc is None else acc + term
        acc = acc.reshape(CONV_ROWS, LANE_CHUNKS, LANES) + bdw_ref[...]
        tmp_ref[i * rows:(i + 1) * rows, :] = acc.reshape(rows, LANES)
        conv = jnp.concatenate(
            [tmp_ref[pl.ds(i * rows + c, CONV_ROWS, stride=LANE_CHUNKS), :] for c in range(LANE_CHUNKS)], axis=1)
        mu = jnp.mean(conv, axis=-1, keepdims=True)
        xc = conv - mu
        var = jnp.mean(xc * xc, axis=-1, keepdims=True)
        y = xc * lax.rsqrt(var + LN_EPS) * lng + lnb
        act_ref[r0:r0 + CONV_ROWS, :] = (_silu(y) * zg_ref[r0:r0 + CONV_ROWS, :]).astype(BF16)

    y_ref[0] = x + _dot(act_ref[...], wout_ref[...])

    hprev_ref[...] = hist_ref[tm * LANE_CHUNKS:total, :]


def _mixer_a_prompt(x, g, w_in, w_dw, b_dw, ln_g, ln_b, w_out):
    b, t, d = x.shape
    tm = TILE_M
    const = lambda shape: pl.BlockSpec(shape, lambda i, j: (0,) * len(shape))
    w_pairs = jnp.tile(w_dw.reshape(CONV_WIDTH, 1, LANE_CHUNKS, LANES), (1, 2, 1, 1))
    w_pairs = w_pairs.reshape(CONV_WIDTH * PAIR_ROWS, LANES).astype(BF16)
    return pl.pallas_call(
        _mixer_a_prompt_kernel,
        grid=(b, t // tm),
        in_specs=[
            pl.BlockSpec((1, tm, d), lambda i, j: (i, j, 0)),
            const((1, d)),
            const((d, 3 * D_CONV)),
            const((CONV_WIDTH * PAIR_ROWS, LANES)),
            const((LANE_CHUNKS, LANES)),
            const((1, D_CONV)),
            const((1, D_CONV)),
            const((D_CONV, d)),
        ],
        out_specs=[
            pl.BlockSpec((1, tm, d), lambda i, j: (i, j, 0)),
            pl.BlockSpec((1, PREFIX_ROWS, D_CONV), lambda i, j: (i, 0, 0)),
        ],
        out_shape=[
            jax.ShapeDtypeStruct((b, t, d), F32),
            jax.ShapeDtypeStruct((b, PREFIX_ROWS, D_CONV), F32),
        ],
        scratch_shapes=[
            pltpu.VMEM(((PREFIX_ROWS + tm) * LANE_CHUNKS, LANES), F32),
            pltpu.VMEM((PREFIX_ROWS * LANE_CHUNKS, LANES), F32),
            pltpu.VMEM(((PREFIX_ROWS + tm) * LANE_CHUNKS, LANES), BF16),
            pltpu.VMEM(((PREFIX_ROWS + tm - 2) * LANE_CHUNKS, LANES), BF16),
            pltpu.VMEM((tm * LANE_CHUNKS, LANES), F32),
            pltpu.VMEM((tm, D_CONV), F32),
            pltpu.VMEM((tm, D_CONV), BF16),
        ],
        compiler_params=pltpu.CompilerParams(
            dimension_semantics=("arbitrary", "arbitrary"), vmem_limit_bytes=VMEM_LIMIT),
        name="mixer_a_prompt",
    )(x, g, w_in, w_pairs, b_dw.reshape(LANE_CHUNKS, LANES), ln_g, ln_b, w_out)


def _mixer_a_sample_kernel(x_ref, st_ref, g_ref, win_ref, wdw_ref, bdw_ref, lng_ref, lnb_ref, wout_ref,
                           y_ref, nst_ref, vf_ref, zg_ref, act_ref):
    nb, ts, d = x_ref.shape
    x = x_ref[...].reshape(nb * ts, d)
    h = (_rms_unit(x) * g_ref[...]).astype(BF16)
    a = _dot(h, win_ref[:, 0:D_CONV])
    gl = _dot(h, win_ref[:, D_CONV:2 * D_CONV])
    v = a * jax.nn.sigmoid(gl)
    zg_ref[...] = _silu(_dot(h, win_ref[:, 2 * D_CONV:3 * D_CONV])).reshape(nb, ts, D_CONV)
    vf_ref[:, 0:CONV_STATE, :] = st_ref[0]
    vf_ref[:, CONV_STATE:CONV_STATE + ts, :] = v.reshape(nb, ts, D_CONV)
    nst_ref[0] = vf_ref[:, ts:ts + CONV_STATE, :]

    bdw = bdw_ref[...]
    lng = lng_ref[...]
    lnb = lnb_ref[...]
    cb = CONV_ROWS // ts

    def chunk(i, carry):
        b0 = pl.multiple_of(i * cb, cb)

        def tap(k):
            return vf_ref[pl.ds(b0, cb), k:k + ts, :].reshape(cb * ts, D_CONV)

        zgate = zg_ref[pl.ds(b0, cb), :, :].reshape(cb * ts, D_CONV)
        gated = _conv_ln_gate(tap, wdw_ref, bdw, lng, lnb, zgate)
        act_ref[pl.ds(b0, cb), :, :] = gated.astype(BF16).reshape(cb, ts, D_CONV)
        return carry

    lax.fori_loop(0, nb // cb, chunk, 0, unroll=2)

    out = _dot(act_ref[...].reshape(nb * ts, D_CONV), wout_ref[...])
    y_ref[...] = (x + out).reshape(nb, ts, d)


def _mixer_a_sample(x, state, g, w_in, w_dw, b_dw, ln_g, ln_b, w_out):
    b, ts, d = x.shape
    nb = SAMPLE_BLOCK
    const = lambda shape: pl.BlockSpec(shape, lambda i: (0,) * len(shape))
    return pl.pallas_call(
        _mixer_a_sample_kernel,
        grid=(b // nb,),
        in_specs=[
            pl.BlockSpec((nb, ts, d), lambda i: (i, 0, 0)),
            pl.BlockSpec((1, nb, CONV_STATE, D_CONV), lambda i: (0, i, 0, 0)),
            const((1, d)),
            const((d, 3 * D_CONV)),
            const((CONV_WIDTH, D_CONV)),
            const((1, D_CONV)),
            const((1, D_CONV)),
            const((1, D_CONV)),
            const((D_CONV, d)),
        ],
        out_specs=[
            pl.BlockSpec((nb, ts, d), lambda i: (i, 0, 0)),
            pl.BlockSpec((1, nb, CONV_STATE, D_CONV), lambda i: (0, i, 0, 0)),
        ],
        out_shape=[
            jax.ShapeDtypeStruct((b, ts, d), F32),
            jax.ShapeDtypeStruct((1, b, CONV_STATE, D_CONV), F32),
        ],
        scratch_shapes=[
            pltpu.VMEM((nb, CONV_STATE + ts + 2, D_CONV), F32),
            pltpu.VMEM((nb, ts, D_CONV), F32),
            pltpu.VMEM((nb, ts, D_CONV), BF16),
        ],
        compiler_params=pltpu.CompilerParams(
            dimension_semantics=("arbitrary",), vmem_limit_bytes=VMEM_LIMIT),
        name="mixer_a_sample",
    )(x, state, g, w_in, w_dw, b_dw, ln_g, ln_b, w_out)


def _project_kvqz(x, gkv_ref, gb_ref, wkv_ref, wq_ref, wz_ref, kg_ref, qg_ref, seg_ref):
    xn = _rms_unit(x)
    hk = (xn * gkv_ref[...]).astype(BF16)
    hq = (xn * gb_ref[...]).astype(BF16)
    seg = seg_ref[...]
    kv = _dot(hk, wkv_ref[...])
    k = _head_rms(kv[:, 0:D_KV], seg, kg_ref[...])
    v = kv[:, D_KV:2 * D_KV]
    q_all = _dot(hq, wq_ref[...])
    q = []
    for g in range(GROUP):
        cols = slice(g * D_KV, (g + 1) * D_KV)
        q.append(_head_rms(q_all[:, cols], seg, qg_ref[...]))
    zg = _silu(_dot(hq, wz_ref[...]))
    return k, v, q, zg


def _lane_segment_mask(kvh):
    lane = lax.broadcasted_iota(jnp.int32, (1, D_KV), 1)
    return (lane >= kvh * HEAD_DIM) & (lane < (kvh + 1) * HEAD_DIM)


def _softmax_unnorm(s, sink):
    m = jnp.maximum(jnp.max(s, axis=-1, keepdims=True), sink)
    p = jnp.exp2(s - m)
    denom = jnp.sum(p, axis=-1, keepdims=True) + jnp.exp2(sink - m)
    return p, denom


def _mixer_b_prompt_kernel(sink_ref, x_ref, gkv_ref, gb_ref, wkv_ref, wq_ref, wz_ref, kg_ref, qg_ref,
                           seg_ref, bias_ref, wout_ref,
                           y_ref, kt_ref, vt_ref,
                           km_ref, vm_ref, kprev_ref, vprev_ref, q_ref, o_ref, hq_ref, zg_ref):
    t = pl.program_id(1)
    tm = x_ref.shape[1]

    zeros = jnp.zeros((N_KV, WINDOW, D_KV), BF16)
    km_ref[:, 0:WINDOW, :] = jnp.where(t > 0, kprev_ref[...], zeros)
    vm_ref[:, 0:WINDOW, :] = jnp.where(t > 0, vprev_ref[...], zeros)

    x = x_ref[0]
    xn = _rms_unit(x)
    hk = (xn * gkv_ref[...]).astype(BF16)
    hq_ref[...] = (xn * gb_ref[...]).astype(BF16)
    seg = seg_ref[...]
    kv = _dot(hk, wkv_ref[...])
    k = _head_rms(kv[:, 0:D_KV], seg, kg_ref[...])
    v = kv[:, D_KV:2 * D_KV]
    for kvh in range(N_KV):
        mask = _lane_segment_mask(kvh)
        km_ref[kvh, WINDOW:WINDOW + tm, :] = jnp.where(mask, k, 0.0).astype(BF16)
        vm_ref[kvh, WINDOW:WINDOW + tm, :] = jnp.where(mask, v, 0.0).astype(BF16)

    @pl.when(t == pl.num_programs(1) - 1)
    def _():
        kt_ref[0] = k[tm - WINDOW:tm, :]
        vt_ref[0] = v[tm - WINDOW:tm, :]

    span = WINDOW + Q_BLOCK

    def project_block(j):
        r0 = j * Q_BLOCK
        hq = hq_ref[r0:r0 + Q_BLOCK, :]
        q_all = _dot(hq, wq_ref[...])
        for g in range(GROUP):
            cols = slice(g * D_KV, (g + 1) * D_KV)
            q_ref[g, r0:r0 + Q_BLOCK, :] = _head_rms(q_all[:, cols], seg, qg_ref[...]).astype(BF16)
        zg_ref[r0:r0 + Q_BLOCK, :] = _silu(_dot(hq, wz_ref[...]))

    def attend_block(j):
        r0 = j * Q_BLOCK
        bsel = jnp.where(t == 0, 0, 1) if j == 0 else 1
        qstack = jnp.concatenate([q_ref[g, r0:r0 + Q_BLOCK, :] for g in range(GROUP)], axis=0)
        out = None
        keys = jnp.concatenate([km_ref[kvh, r0:r0 + span, :] for kvh in range(N_KV)], axis=0)
        s_all = _dot_nt(qstack, keys)
        for kvh in range(N_KV):
            s = s_all[:, kvh * span:(kvh + 1) * span] + bias_ref[bsel, kvh]
            probs = []
            for g in range(GROUP):
                p, denom = _softmax_unnorm(s[g * Q_BLOCK:(g + 1) * Q_BLOCK, :], sink_ref[kvh * GROUP + g])
                probs.append((p * (1.0 / denom)).astype(BF16))
            o = _dot(jnp.concatenate(probs, axis=0), vm_ref[kvh, r0:r0 + span, :])
            out = o if out is None else out + o
        for g in range(GROUP):
            o_ref[r0:r0 + Q_BLOCK, g * D_KV:(g + 1) * D_KV] = out[g * Q_BLOCK:(g + 1) * Q_BLOCK, :]

    def output_block(j):
        rows = slice(j * Q_BLOCK, (j + 1) * Q_BLOCK)
        gated = (o_ref[rows, :] * zg_ref[rows, :]).astype(BF16)
        y_ref[0, rows, :] = x_ref[0, rows, :] + _dot(gated, wout_ref[...])

    n_blocks = tm // Q_BLOCK
    project_block(0)
    for j in range(n_blocks):
        if j + 1 < n_blocks:
            project_block(j + 1)
        attend_block(j)
        if j > 0:
            output_block(j - 1)
    output_block(n_blocks - 1)

    for kvh in range(N_KV):
        mask = _lane_segment_mask(kvh)
        kprev_ref[kvh] = jnp.where(mask, k[tm - WINDOW:tm, :], 0.0).astype(BF16)
        vprev_ref[kvh] = jnp.where(mask, v[tm - WINDOW:tm, :], 0.0).astype(BF16)


def _mixer_b_prompt(x, sinks, gkv, gb, wkv, wq, wz, kg, qg, seg, bias, wout):
    b, t, d = x.shape
    tm = TILE_M
    const = lambda shape: pl.BlockSpec(shape, lambda i, j: (0,) * len(shape), pipeline_mode=pl.Buffered(1))
    return pl.pallas_call(
        _mixer_b_prompt_kernel,
        grid=(b, t // tm),
        in_specs=[
            pl.BlockSpec(memory_space=pltpu.SMEM),
            pl.BlockSpec((1, tm, d), lambda i, j: (i, j, 0)),
            const((1, d)),
            const((1, d)),
            const((d, 2 * D_KV)),
            const((d, D_ATTN)),
            const((d, D_ATTN)),
            const((1, D_KV)),
            const((1, D_KV)),
            const((D_KV, D_KV)),
            const((2, N_KV, GROUP * Q_BLOCK, WINDOW + Q_BLOCK)),
            const((D_ATTN, d)),
        ],
        out_specs=[
            pl.BlockSpec((1, tm, d), lambda i, j: (i, j, 0)),
            pl.BlockSpec((1, WINDOW, D_KV), lambda i, j: (i, 0, 0)),
            pl.BlockSpec((1, WINDOW, D_KV), lambda i, j: (i, 0, 0)),
        ],
        out_shape=[
            jax.ShapeDtypeStruct((b, t, d), F32),
            jax.ShapeDtypeStruct((b, WINDOW, D_KV), F32),
            jax.ShapeDtypeStruct((b, WINDOW, D_KV), F32),
        ],
        scratch_shapes=[
            pltpu.VMEM((N_KV, WINDOW + tm, D_KV), BF16),
            pltpu.VMEM((N_KV, WINDOW + tm, D_KV), BF16),
            pltpu.VMEM((N_KV, WINDOW, D_KV), BF16),
            pltpu.VMEM((N_KV, WINDOW, D_KV), BF16),
            pltpu.VMEM((GROUP, tm, D_KV), BF16),
            pltpu.VMEM((tm, D_ATTN), F32),
            pltpu.VMEM((tm, d), BF16),
            pltpu.VMEM((tm, D_ATTN), F32),
        ],
        compiler_params=pltpu.CompilerParams(
            dimension_semantics=("arbitrary", "arbitrary"), vmem_limit_bytes=VMEM_LIMIT),
        name="mixer_b_prompt",
    )(sinks, x, gkv, gb, wkv, wq, wz, kg, qg, seg, bias, wout)


def _mixer_b_sample_kernel(x_ref, ck_ref, cv_ref, gkv_ref, gb_ref, wkv_ref, wq_ref, wz_ref, kg_ref,
                           qg_ref, seg_ref, bias_ref, sink_ref, wout_ref,
                           y_ref, nk_ref, nv_ref,
                           kc_ref, vc_ref, q_ref, o_ref):
    nb, ts, d = x_ref.shape
    x = x_ref[...].reshape(nb * ts, d)
    k, v, q, zg = _project_kvqz(x, gkv_ref, gb_ref, wkv_ref, wq_ref, wz_ref, kg_ref, qg_ref, seg_ref)
    for g in range(GROUP):
        q_ref[g] = q[g].reshape(nb, ts, D_KV)
    ctx = kc_ref.shape[1]
    kc_ref[:, 0:WINDOW, :] = ck_ref[...]
    vc_ref[:, 0:WINDOW, :] = cv_ref[...]
    kc_ref[:, WINDOW:WINDOW + ts, :] = k.reshape(nb, ts, D_KV)
    vc_ref[:, WINDOW:WINDOW + ts, :] = v.reshape(nb, ts, D_KV)
    kc_ref[:, WINDOW + ts:ctx, :] = jnp.zeros((nb, ctx - WINDOW - ts, D_KV), F32)
    vc_ref[:, WINDOW + ts:ctx, :] = jnp.zeros((nb, ctx - WINDOW - ts, D_KV), F32)
    nk_ref[...] = kc_ref[:, ts:ts + WINDOW, :]
    nv_ref[...] = vc_ref[:, ts:ts + WINDOW, :]

    masks = [_lane_segment_mask(kvh) for kvh in range(N_KV)]
    bias = bias_ref[...]
    sink = sink_ref[...]

    def seq(i, carry):
        kk = kc_ref[i].astype(BF16)
        vv = vc_ref[i].astype(BF16)
        rows = []
        for g in range(GROUP):
            qg_rows = q_ref[g, i]
            for kvh in range(N_KV):
                rows.append(jnp.where(masks[kvh], qg_rows, jnp.zeros_like(qg_rows)))
        lhs = jnp.concatenate(rows, axis=0).astype(BF16)
        s = _dot_nt(lhs, kk) + bias
        p, denom = _softmax_unnorm(s, sink)
        o = _dot((p * (1.0 / denom)).astype(BF16), vv)
        for g in range(GROUP):
            acc = None
            for kvh in range(N_KV):
                r = (g * N_KV + kvh) * ts
                part = jnp.where(masks[kvh], o[r:r + ts, :], 0.0)
                acc = part if acc is None else acc + part
            o_ref[i, :, g * D_KV:(g + 1) * D_KV] = acc
        return carry

    lax.fori_loop(0, nb, seq, 0, unroll=SEQ_UNROLL)

    og = (o_ref[...].reshape(nb * ts, D_ATTN) * zg).astype(BF16)
    y_ref[...] = (x + _dot(og, wout_ref[...])).reshape(nb, ts, d)


def _mixer_b_sample(x, ck, cv, gkv, gb, wkv, wq, wz, kg, qg, seg, bias, sink, wout):
    b, ts, d = x.shape
    nb = SAMPLE_BLOCK_ATTN
    const = lambda shape: pl.BlockSpec(shape, lambda i: (0,) * len(shape))
    rows = N_HEADS * ts
    ctx = bias.shape[1]
    return pl.pallas_call(
        _mixer_b_sample_kernel,
        grid=(b // nb,),
        in_specs=[
            pl.BlockSpec((nb, ts, d), lambda i: (i, 0, 0)),
            pl.BlockSpec((nb, WINDOW, D_KV), lambda i: (i, 0, 0)),
            pl.BlockSpec((nb, WINDOW, D_KV), lambda i: (i, 0, 0)),
            const((1, d)),
            const((1, d)),
            const((d, 2 * D_KV)),
            const((d, D_ATTN)),
            const((d, D_ATTN)),
            const((1, D_KV)),
            const((1, D_KV)),
            const((D_KV, D_KV)),
            const((rows, ctx)),
            const((rows, 1)),
            const((D_ATTN, d)),
        ],
        out_specs=[
            pl.BlockSpec((nb, ts, d), lambda i: (i, 0, 0)),
            pl.BlockSpec((nb, WINDOW, D_KV), lambda i: (i, 0, 0)),
            pl.BlockSpec((nb, WINDOW, D_KV), lambda i: (i, 0, 0)),
        ],
        out_shape=[
            jax.ShapeDtypeStruct((b, ts, d), F32),
            jax.ShapeDtypeStruct((b, WINDOW, D_KV), F32),
            jax.ShapeDtypeStruct((b, WINDOW, D_KV), F32),
        ],
        scratch_shapes=[
            pltpu.VMEM((nb, ctx, D_KV), F32),
            pltpu.VMEM((nb, ctx, D_KV), F32),
            pltpu.VMEM((GROUP, nb, ts, D_KV), F32),
            pltpu.VMEM((nb, ts, D_ATTN), F32),
        ],
        compiler_params=pltpu.CompilerParams(
            dimension_semantics=("arbitrary",), vmem_limit_bytes=VMEM_LIMIT),
        name="mixer_b_sample",
    )(x, ck, cv, gkv, gb, wkv, wq, wz, kg, qg, seg, bias, sink, wout)


def _alibi_slopes():
    return 2.0 ** (-8.0 * np.arange(1, N_HEADS + 1) / N_HEADS)


def _band_bias(n_query, n_key, first_block):
    dist = np.arange(n_query)[:, None] + WINDOW - np.arange(n_key)[None, :]
    allowed = (dist >= 0) & (dist < WINDOW)
    if first_block:
        allowed = allowed & (np.arange(n_key)[None, :] >= WINDOW)
    bias = -_alibi_slopes()[:, None, None] * dist[None].astype(np.float64) * LOG2E
    return np.where(allowed[None], bias, -np.inf).astype(np.float32)


def _group_major(w, axis):
    shape = w.shape
    w = w.reshape(shape[:axis] + (N_KV, GROUP, HEAD_DIM) + shape[axis + 1:])
    w = jnp.swapaxes(w, axis, axis + 1)
    return w.reshape(shape)


def kernel(x_prompt, x_sample, state_conv, cache_k, cache_v, norm_a, w_in_a, w_dw_a, b_dw_a, ln_g_a, ln_b_a, w_out_a, norm_kv, w_kv, k_norm, norm_b, w_in_b, q_norm, sinks_b, w_out_b):
    assert norm_a.shape[0] == 1 and norm_b.shape[0] == 1
    row = lambda v: v.reshape(1, -1).astype(F32)

    a_args = (row(norm_a[0]), w_in_a[0].astype(BF16), w_dw_a[0].astype(F32), row(b_dw_a[0]),
              row(ln_g_a[0]), row(ln_b_a[0]), w_out_a[0].astype(BF16))
    xp, st_p = _mixer_a_prompt(x_prompt, *a_args)
    xs, st_s = _mixer_a_sample(x_sample, state_conv, *a_args)

    wq = _group_major(w_in_b[0][:, :D_ATTN], 1).astype(BF16)
    wz = _group_major(w_in_b[0][:, D_ATTN:], 1).astype(BF16)
    wout = _group_major(w_out_b[0], 0).astype(BF16)
    sinks = sinks_b[0].astype(F32) * LOG2E
    kg = row(jnp.tile(k_norm, N_KV))
    qg = row(jnp.tile(q_norm[0], N_KV)) * (HEAD_DIM ** -0.5 * LOG2E)
    seg = jnp.asarray(np.kron(np.eye(N_KV), np.full((HEAD_DIM, HEAD_DIM), 1.0 / HEAD_DIM)), BF16)
    shared = (row(norm_kv), row(norm_b[0]), w_kv.astype(BF16), wq, wz, kg, qg, seg)

    span = WINDOW + Q_BLOCK
    bias_p = jnp.asarray(np.stack([_band_bias(Q_BLOCK, span, True), _band_bias(Q_BLOCK, span, False)])
                         .reshape(2, N_KV, GROUP * Q_BLOCK, span))
    yp, k_tail, v_tail = _mixer_b_prompt(xp, sinks, *shared, bias_p, wout)

    ts = x_sample.shape[1]
    head_of_row = np.array([kvh * GROUP + g for g in range(GROUP) for kvh in range(N_KV)])
    ctx = -(-(WINDOW + ts) // BF16_ROWS) * BF16_ROWS
    bias_s = jnp.asarray(_band_bias(ts, ctx, False)[head_of_row].reshape(N_HEADS * ts, ctx))
    sink_s = jnp.repeat(sinks[head_of_row], ts).reshape(N_HEADS * ts, 1)
    nb = cache_k.shape[0]
    ys, nk, nv = _mixer_b_sample(xs, cache_k.reshape(nb, WINDOW, D_KV), cache_v.reshape(nb, WINDOW, D_KV),
                                 *shared, bias_s, sink_s, wout)

    kv4 = lambda a: a.reshape(a.shape[0], WINDOW, N_KV, HEAD_DIM)
    return (yp, ys, st_p[None, :, PREFIX_SKIP:, :], st_s, kv4(k_tail), kv4(v_tail), kv4(nk), kv4(nv))
```

```python
import numpy as np
import jax
import jax.numpy as jnp
from jax import lax
from jax.experimental import pallas as pl
from jax.experimental.pallas import tpu as pltpu

F32 = jnp.float32
BF16 = jnp.bfloat16

D_MODEL = 1024
D_CONV = 1024
CONV_WIDTH = 31
CONV_STATE = CONV_WIDTH - 1
N_HEADS = 16
N_KV = 4
GROUP = N_HEADS // N_KV
HEAD_DIM = 64
D_ATTN = N_HEADS * HEAD_DIM
D_KV = N_KV * HEAD_DIM
WINDOW = 128
Q_BLOCK = 128
RMS_EPS = 1e-6
LN_EPS = 1e-5
LOG2E = float(np.log2(np.e))

SUBLANES = 8
LANES = 128
LANE_CHUNKS = D_CONV // LANES
assert LANE_CHUNKS == SUBLANES
BF16_ROWS = 16
PAIR_ROWS = 2 * LANE_CHUNKS
assert PAIR_ROWS == BF16_ROWS
PREFIX_ROWS = 32
PREFIX_SKIP = PREFIX_ROWS - CONV_STATE
TILE_M = 1024
CONV_ROWS = 32
SAMPLE_BLOCK = 32
SAMPLE_BLOCK_ATTN = 16
NEW_KEY_ROWS = 16
SEQ_UNROLL = 4
VMEM_LIMIT = 56 * 1024 * 1024


def _dot(a, b):
    return jnp.dot(a, b, preferred_element_type=F32)


def _dot_nt(a, b):
    return lax.dot_general(a, b, (((1,), (1,)), ((), ())), preferred_element_type=F32)


def _rms_unit(x):
    return x * lax.rsqrt(jnp.mean(x * x, axis=-1, keepdims=True) + RMS_EPS)


def _silu(x):
    return x * jax.nn.sigmoid(x)


def _head_rms(x, seg_mean, gain):
    ms = _dot((x * x).astype(BF16), seg_mean)
    return x * lax.rsqrt(ms + RMS_EPS) * gain


def _conv_ln_gate(tap, wdw_ref, bdw, lng, lnb, zgate):
    acc = tap(0) * wdw_ref[0:1, :] + bdw
    for k in range(1, CONV_WIDTH):
        acc = acc + tap(k) * wdw_ref[k:k + 1, :]
    mu = jnp.mean(acc, axis=-1, keepdims=True)
    xc = acc - mu
    var = jnp.mean(xc * xc, axis=-1, keepdims=True)
    y = xc * lax.rsqrt(var + LN_EPS) * lng + lnb
    return _silu(y) * zgate


def _mixer_a_prompt_kernel(x_ref, g_ref, win_ref, w16_ref, bdw_ref, lng_ref, lnb_ref, wout_ref,
                           y_ref, st_ref, hist_ref, hprev_ref, even_ref, odd_ref, tmp_ref, zg_ref, act_ref):
    t = pl.program_id(1)
    tm = x_ref.shape[1]
    prefix = PREFIX_ROWS * LANE_CHUNKS
    total = prefix + tm * LANE_CHUNKS

    hist_ref[0:prefix, :] = jnp.where(t > 0, hprev_ref[...], jnp.zeros((prefix, LANES), F32))

    x = x_ref[0]
    h = (_rms_unit(x) * g_ref[...]).astype(BF16)
    a = _dot(h, win_ref[:, 0:D_CONV])
    gl = _dot(h, win_ref[:, D_CONV:2 * D_CONV])
    v = a * jax.nn.sigmoid(gl)
    for c in range(LANE_CHUNKS):
        hist_ref[pl.ds(prefix + c, tm, stride=LANE_CHUNKS), :] = v[:, c * LANES:(c + 1) * LANES]
    zg_ref[...] = _silu(_dot(h, win_ref[:, 2 * D_CONV:3 * D_CONV]))

    @pl.when(t == pl.num_programs(1) - 1)
    def _():
        st_ref[0] = v[tm - PREFIX_ROWS:tm, :]

    even_ref[...] = hist_ref[...].astype(BF16)
    odd_ref[...] = hist_ref[LANE_CHUNKS:total - LANE_CHUNKS, :].astype(BF16)

    lng = lng_ref[...]
    lnb = lnb_ref[...]
    rows = CONV_ROWS * LANE_CHUNKS
    for i in range(tm // CONV_ROWS):
        r0 = i * CONV_ROWS
        acc = None
        for k in range(CONV_WIDTH):
            tau = r0 + PREFIX_SKIP + k
            src, row = (even_ref, tau * LANE_CHUNKS) if tau % 2 == 0 else (odd_ref, (tau - 1) * LANE_CHUNKS)
            tap = src[row:row + rows, :].astype(F32).reshape(CONV_ROWS // 2, PAIR_ROWS, LANES)
            term = tap * w16_ref[k * PAIR_ROWS:(k + 1) * PAIR_ROWS, :].astype(F32)
            acc = term if acc is None else acc + term
        acc = acc.reshape(CONV_ROWS, LANE_CHUNKS, LANES) + bdw_ref[...]
        tmp_ref[i * rows:(i + 1) * rows, :] = acc.reshape(rows, LANES)
        conv = jnp.concatenate(
            [tmp_ref[pl.ds(i * rows + c, CONV_ROWS, stride=LANE_CHUNKS), :] for c in range(LANE_CHUNKS)], axis=1)
        mu = jnp.mean(conv, axis=-1, keepdims=True)
        xc = conv - mu
        var = jnp.mean(xc * xc, axis=-1, keepdims=True)
        y = xc * lax.rsqrt(var + LN_EPS) * lng + lnb
        act_ref[r0:r0 + CONV_ROWS, :] = (_silu(y) * zg_ref[r0:r0 + CONV_ROWS, :]).astype(BF16)

    y_ref[0] = x + _dot(act_ref[...], wout_ref[...])

    hprev_ref[...] = hist_ref[tm * LANE_CHUNKS:total, :]


def _mixer_a_prompt(x, g, w_in, w_dw, b_dw, ln_g, ln_b, w_out):
    b, t, d = x.shape
    tm = TILE_M
    const = lambda shape: pl.BlockSpec(shape, lambda i, j: (0,) * len(shape))
    w_pairs = jnp.tile(w_dw.reshape(CONV_WIDTH, 1, LANE_CHUNKS, LANES), (1, 2, 1, 1))
    w_pairs = w_pairs.reshape(CONV_WIDTH * PAIR_ROWS, LANES).astype(BF16)
    return pl.pallas_call(
        _mixer_a_prompt_kernel,
        grid=(b, t // tm),
        in_specs=[
            pl.BlockSpec((1, tm, d), lambda i, j: (i, j, 0)),
            const((1, d)),
            const((d, 3 * D_CONV)),
            const((CONV_WIDTH * PAIR_ROWS, LANES)),
            const((LANE_CHUNKS, LANES)),
            const((1, D_CONV)),
            const((1, D_CONV)),
            const((D_CONV, d)),
        ],
        out_specs=[
            pl.BlockSpec((1, tm, d), lambda i, j: (i, j, 0)),
            pl.BlockSpec((1, PREFIX_ROWS, D_CONV), lambda i, j: (i, 0, 0)),
        ],
        out_shape=[
            jax.ShapeDtypeStruct((b, t, d), F32),
            jax.ShapeDtypeStruct((b, PREFIX_ROWS, D_CONV), F32),
        ],
        scratch_shapes=[
            pltpu.VMEM(((PREFIX_ROWS + tm) * LANE_CHUNKS, LANES), F32),
            pltpu.VMEM((PREFIX_ROWS * LANE_CHUNKS, LANES), F32),
            pltpu.VMEM(((PREFIX_ROWS + tm) * LANE_CHUNKS, LANES), BF16),
            pltpu.VMEM(((PREFIX_ROWS + tm - 2) * LANE_CHUNKS, LANES), BF16),
            pltpu.VMEM((tm * LANE_CHUNKS, LANES), F32),
            pltpu.VMEM((tm, D_CONV), F32),
            pltpu.VMEM((tm, D_CONV), BF16),
        ],
        compiler_params=pltpu.CompilerParams(
            dimension_semantics=("arbitrary", "arbitrary"), vmem_limit_bytes=VMEM_LIMIT),
        name="mixer_a_prompt",
    )(x, g, w_in, w_pairs, b_dw.reshape(LANE_CHUNKS, LANES), ln_g, ln_b, w_out)


def _mixer_a_sample_kernel(x_ref, st_ref, g_ref, win_ref, wdw_ref, bdw_ref, lng_ref, lnb_ref, wout_ref,
                           y_ref, nst_ref, vf_ref, zg_ref, act_ref):
    nb, ts, d = x_ref.shape
    x = x_ref[...].reshape(nb * ts, d)
    h = (_rms_unit(x) * g_ref[...]).astype(BF16)
    a = _dot(h, win_ref[:, 0:D_CONV])
    gl = _dot(h, win_ref[:, D_CONV:2 * D_CONV])
    v = a * jax.nn.sigmoid(gl)
    zg_ref[...] = _silu(_dot(h, win_ref[:, 2 * D_CONV:3 * D_CONV])).reshape(nb, ts, D_CONV)
    vf_ref[:, 0:CONV_STATE, :] = st_ref[0]
    vf_ref[:, CONV_STATE:CONV_STATE + ts, :] = v.reshape(nb, ts, D_CONV)
    nst_ref[0] = vf_ref[:, ts:ts + CONV_STATE, :]

    bdw = bdw_ref[...]
    lng = lng_ref[...]
    lnb = lnb_ref[...]
    cb = CONV_ROWS // ts

    def chunk(i, carry):
        b0 = pl.multiple_of(i * cb, cb)

        def tap(k):
            return vf_ref[pl.ds(b0, cb), k:k + ts, :].reshape(cb * ts, D_CONV)

        zgate = zg_ref[pl.ds(b0, cb), :, :].reshape(cb * ts, D_CONV)
        gated = _conv_ln_gate(tap, wdw_ref, bdw, lng, lnb, zgate)
        act_ref[pl.ds(b0, cb), :, :] = gated.astype(BF16).reshape(cb, ts, D_CONV)
        return carry

    lax.fori_loop(0, nb // cb, chunk, 0, unroll=2)

    out = _dot(act_ref[...].reshape(nb * ts, D_CONV), wout_ref[...])
    y_ref[...] = (x + out).reshape(nb, ts, d)


def _mixer_a_sample(x, state, g, w_in, w_dw, b_dw, ln_g, ln_b, w_out):
    b, ts, d = x.shape
    nb = SAMPLE_BLOCK
    const = lambda shape: pl.BlockSpec(shape, lambda i: (0,) * len(shape))
    return pl.pallas_call(
        _mixer_a_sample_kernel,
        grid=(b // nb,),
        in_specs=[
            pl.BlockSpec((nb, ts, d), lambda i: (i, 0, 0)),
            pl.BlockSpec((1, nb, CONV_STATE, D_CONV), lambda i: (0, i, 0, 0)),
            const((1, d)),
            const((d, 3 * D_CONV)),
            const((CONV_WIDTH, D_CONV)),
            const((1, D_CONV)),
            const((1, D_CONV)),
            const((1, D_CONV)),
            const((D_CONV, d)),
        ],
        out_specs=[
            pl.BlockSpec((nb, ts, d), lambda i: (i, 0, 0)),
            pl.BlockSpec((1, nb, CONV_STATE, D_CONV), lambda i: (0, i, 0, 0)),
        ],
        out_shape=[
            jax.ShapeDtypeStruct((b, ts, d), F32),
            jax.ShapeDtypeStruct((1, b, CONV_STATE, D_CONV), F32),
        ],
        scratch_shapes=[
            pltpu.VMEM((nb, CONV_STATE + ts + 2, D_CONV), F32),
            pltpu.VMEM((nb, ts, D_CONV), F32),
            pltpu.VMEM((nb, ts, D_CONV), BF16),
        ],
        compiler_params=pltpu.CompilerParams(
            dimension_semantics=("arbitrary",), vmem_limit_bytes=VMEM_LIMIT),
        name="mixer_a_sample",
    )(x, state, g, w_in, w_dw, b_dw, ln_g, ln_b, w_out)


def _project_kvqz(x, gkv_ref, gb_ref, wkv_ref, wq_ref, wz_ref, kg_ref, qg_ref, seg_ref):
    xn = _rms_unit(x)
    hk = (xn * gkv_ref[...]).astype(BF16)
    hq = (xn * gb_ref[...]).astype(BF16)
    seg = seg_ref[...]
    kv = _dot(hk, wkv_ref[...])
    k = _head_rms(kv[:, 0:D_KV], seg, kg_ref[...])
    v = kv[:, D_KV:2 * D_KV]
    q_all = _dot(hq, wq_ref[...])
    q = []
    for g in range(GROUP):
        cols = slice(g * D_KV, (g + 1) * D_KV)
        q.append(_head_rms(q_all[:, cols], seg, qg_ref[...]))
    zg = _silu(_dot(hq, wz_ref[...]))
    return k, v, q, zg


def _lane_segment_mask(kvh):
    lane = lax.broadcasted_iota(jnp.int32, (1, D_KV), 1)
    return (lane >= kvh * HEAD_DIM) & (lane < (kvh + 1) * HEAD_DIM)


def _softmax_unnorm(s, sink):
    m = jnp.maximum(jnp.max(s, axis=-1, keepdims=True), sink)
    p = jnp.exp2(s - m)
    denom = jnp.sum(p, axis=-1, keepdims=True) + jnp.exp2(sink - m)
    return p, denom


def _mixer_b_prompt_kernel(sink_ref, x_ref, gkv_ref, gb_ref, wkv_ref, wq_ref, wz_ref, kg_ref, qg_ref,
                           seg_ref, bias_ref, wout_ref,
                           y_ref, kt_ref, vt_ref,
                           km_ref, vm_ref, kprev_ref, vprev_ref, q_ref, o_ref, hq_ref, zg_ref):
    t = pl.program_id(1)
    tm = x_ref.shape[1]

    zeros = jnp.zeros((N_KV, WINDOW, D_KV), BF16)
    km_ref[:, 0:WINDOW, :] = jnp.where(t > 0, kprev_ref[...], zeros)
    vm_ref[:, 0:WINDOW, :] = jnp.where(t > 0, vprev_ref[...], zeros)

    x = x_ref[0]
    xn = _rms_unit(x)
    hk = (xn * gkv_ref[...]).astype(BF16)
    hq_ref[...] = (xn * gb_ref[...]).astype(BF16)
    seg = seg_ref[...]
    kv = _dot(hk, wkv_ref[...])
    k = _head_rms(kv[:, 0:D_KV], seg, kg_ref[...])
    v = kv[:, D_KV:2 * D_KV]
    for kvh in range(N_KV):
        mask = _lane_segment_mask(kvh)
        km_ref[kvh, WINDOW:WINDOW + tm, :] = jnp.where(mask, k, 0.0).astype(BF16)
        vm_ref[kvh, WINDOW:WINDOW + tm, :] = jnp.where(mask, v, 0.0).astype(BF16)

    @pl.when(t == pl.num_programs(1) - 1)
    def _():
        kt_ref[0] = k[tm - WINDOW:tm, :]
        vt_ref[0] = v[tm - WINDOW:tm, :]

    span = WINDOW + Q_BLOCK

    def project_block(j):
        r0 = j * Q_BLOCK
        hq = hq_ref[r0:r0 + Q_BLOCK, :]
        q_all = _dot(hq, wq_ref[...])
        for g in range(GROUP):
            cols = slice(g * D_KV, (g + 1) * D_KV)
            q_ref[g, r0:r0 + Q_BLOCK, :] = _head_rms(q_all[:, cols], seg, qg_ref[...]).astype(BF16)
        zg_ref[r0:r0 + Q_BLOCK, :] = _silu(_dot(hq, wz_ref[...]))

    def attend_block(j):
        r0 = j * Q_BLOCK
        bsel = jnp.where(t == 0, 0, 1) if j == 0 else 1
        qstack = jnp.concatenate([q_ref[g, r0:r0 + Q_BLOCK, :] for g in range(GROUP)], axis=0)
        out = None
        keys = jnp.concatenate([km_ref[kvh, r0:r0 + span, :] for kvh in range(N_KV)], axis=0)
        s_all = _dot_nt(qstack, keys)
        for kvh in range(N_KV):
            s = s_all[:, kvh * span:(kvh + 1) * span] + bias_ref[bsel, kvh]
            probs = []
            for g in range(GROUP):
                p, denom = _softmax_unnorm(s[g * Q_BLOCK:(g + 1) * Q_BLOCK, :], sink_ref[kvh * GROUP + g])
                probs.append((p * (1.0 / denom)).astype(BF16))
            o = _dot(jnp.concatenate(probs, axis=0), vm_ref[kvh, r0:r0 + span, :])
            out = o if out is None else out + o
        for g in range(GROUP):
            o_ref[r0:r0 + Q_BLOCK, g * D_KV:(g + 1) * D_KV] = out[g * Q_BLOCK:(g + 1) * Q_BLOCK, :]

    def output_block(j):
        rows = slice(j * Q_BLOCK, (j + 1) * Q_BLOCK)
        gated = (o_ref[rows, :] * zg_ref[rows, :]).astype(BF16)
        y_ref[0, rows, :] = x_ref[0, rows, :] + _dot(gated, wout_ref[...])

    n_blocks = tm // Q_BLOCK
    project_block(0)
    for j in range(n_blocks):
        if j + 1 < n_blocks:
            project_block(j + 1)
        attend_block(j)
        if j > 0:
            output_block(j - 1)
    output_block(n_blocks - 1)

    for kvh in range(N_KV):
        mask = _lane_segment_mask(kvh)
        kprev_ref[kvh] = jnp.where(mask, k[tm - WINDOW:tm, :], 0.0).astype(BF16)
        vprev_ref[kvh] = jnp.where(mask, v[tm - WINDOW:tm, :], 0.0).astype(BF16)


def _mixer_b_prompt(x, sinks, gkv, gb, wkv, wq, wz, kg, qg, seg, bias, wout):
    b, t, d = x.shape
    tm = TILE_M
    const = lambda shape: pl.BlockSpec(shape, lambda i, j: (0,) * len(shape), pipeline_mode=pl.Buffered(1))
    return pl.pallas_call(
        _mixer_b_prompt_kernel,
        grid=(b, t // tm),
        in_specs=[
            pl.BlockSpec(memory_space=pltpu.SMEM),
            pl.BlockSpec((1, tm, d), lambda i, j: (i, j, 0)),
            const((1, d)),
            const((1, d)),
            const((d, 2 * D_KV)),
            const((d, D_ATTN)),
            const((d, D_ATTN)),
            const((1, D_KV)),
            const((1, D_KV)),
            const((D_KV, D_KV)),
            const((2, N_KV, GROUP * Q_BLOCK, WINDOW + Q_BLOCK)),
            const((D_ATTN, d)),
        ],
        out_specs=[
            pl.BlockSpec((1, tm, d), lambda i, j: (i, j, 0)),
            pl.BlockSpec((1, WINDOW, D_KV), lambda i, j: (i, 0, 0)),
            pl.BlockSpec((1, WINDOW, D_KV), lambda i, j: (i, 0, 0)),
        ],
        out_shape=[
            jax.ShapeDtypeStruct((b, t, d), F32),
            jax.ShapeDtypeStruct((b, WINDOW, D_KV), F32),
            jax.ShapeDtypeStruct((b, WINDOW, D_KV), F32),
        ],
        scratch_shapes=[
            pltpu.VMEM((N_KV, WINDOW + tm, D_KV), BF16),
            pltpu.VMEM((N_KV, WINDOW + tm, D_KV), BF16),
            pltpu.VMEM((N_KV, WINDOW, D_KV), BF16),
            pltpu.VMEM((N_KV, WINDOW, D_KV), BF16),
            pltpu.VMEM((GROUP, tm, D_KV), BF16),
            pltpu.VMEM((tm, D_ATTN), F32),
            pltpu.VMEM((tm, d), BF16),
            pltpu.VMEM((tm, D_ATTN), F32),
        ],
        compiler_params=pltpu.CompilerParams(
            dimension_semantics=("arbitrary", "arbitrary"), vmem_limit_bytes=VMEM_LIMIT),
        name="mixer_b_prompt",
    )(sinks, x, gkv, gb, wkv, wq, wz, kg, qg, seg, bias, wout)


def _mixer_b_sample_kernel(x_ref, ck_ref, cv_ref, gkv_ref, gb_ref, wkv_ref, wq_ref, wz_ref, kg_ref,
                           qg_ref, seg_ref, biasc_ref, biasn_ref, sink_ref, wout_ref,
                           y_ref, nk_ref, nv_ref,
                           kn_ref, vn_ref, q_ref, o_ref):
    nb, ts, d = x_ref.shape
    x = x_ref[...].reshape(nb * ts, d)
    k, v, q, zg = _project_kvqz(x, gkv_ref, gb_ref, wkv_ref, wq_ref, wz_ref, kg_ref, qg_ref, seg_ref)
    for g in range(GROUP):
        q_ref[g] = q[g].reshape(nb, ts, D_KV)
    kn_ref[...] = k.reshape(nb, ts, D_KV)
    vn_ref[...] = v.reshape(nb, ts, D_KV)

    masks = [_lane_segment_mask(kvh) for kvh in range(N_KV)]
    bias_c = biasc_ref[...]
    bias_n = biasn_ref[...]
    sink = sink_ref[...]
    new_lanes = lax.broadcasted_iota(jnp.int32, (1, WINDOW), 1) >= WINDOW - ts
    pad_rows = jnp.zeros((NEW_KEY_ROWS - ts, D_KV), F32)
    head_rows = jnp.zeros((WINDOW - ts, D_KV), F32)

    def seq(i, carry):
        k_old = ck_ref[i]
        v_old = cv_ref[i]
        k_new = kn_ref[i]
        v_new = vn_ref[i]
        rows = []
        for g in range(GROUP):
            qg_rows = q_ref[g, i]
            for kvh in range(N_KV):
                rows.append(jnp.where(masks[kvh], qg_rows, jnp.zeros_like(qg_rows)))
        lhs = jnp.concatenate(rows, axis=0).astype(BF16)
        k_new16 = jnp.concatenate([k_new, pad_rows], axis=0).astype(BF16)
        v_new16 = jnp.concatenate([v_new, pad_rows], axis=0).astype(BF16)
        s_c = _dot(lhs, k_old.astype(BF16)) + bias_c
        s_n = _dot_nt(lhs, k_new16) + bias_n
        m = jnp.maximum(jnp.maximum(jnp.max(s_c, axis=-1, keepdims=True), jnp.max(s_n, axis=-1, keepdims=True)),
                        sink)
        p_c = jnp.exp2(s_c - m)
        p_n = jnp.exp2(s_n - m)
        denom = (jnp.sum(p_c, axis=-1, keepdims=True) + jnp.sum(p_n, axis=-1, keepdims=True)
                 + jnp.exp2(sink - m))
        r = 1.0 / denom
        o = _dot_nt((p_c * r).astype(BF16), v_old.astype(BF16)) + _dot((p_n * r).astype(BF16), v_new16)
        for g in range(GROUP):
            acc = None
            for kvh in range(N_KV):
                r0 = (g * N_KV + kvh) * ts
                part = jnp.where(masks[kvh], o[r0:r0 + ts, :], 0.0)
                acc = part if acc is None else acc + part
            o_ref[i, :, g * D_KV:(g + 1) * D_KV] = acc
        k_cols = jnp.concatenate([head_rows, k_new], axis=0).T
        v_cols = jnp.concatenate([head_rows, v_new], axis=0).T
        nk_ref[i] = jnp.where(new_lanes, k_cols, pltpu.roll(k_old, WINDOW - ts, axis=1))
        nv_ref[i] = jnp.where(new_lanes, v_cols, pltpu.roll(v_old, WINDOW - ts, axis=1))
        return carry

    lax.fori_loop(0, nb, seq, 0, unroll=SEQ_UNROLL)

    og = (o_ref[...].reshape(nb * ts, D_ATTN) * zg).astype(BF16)
    y_ref[...] = (x + _dot(og, wout_ref[...])).reshape(nb, ts, d)


def _mixer_b_sample(x, ck_t, cv_t, gkv, gb, wkv, wq, wz, kg, qg, seg, bias_c, bias_n, sink, wout):
    b, ts, d = x.shape
    nb = SAMPLE_BLOCK_ATTN
    const = lambda shape: pl.BlockSpec(shape, lambda i: (0,) * len(shape))
    rows = N_HEADS * ts
    return pl.pallas_call(
        _mixer_b_sample_kernel,
        grid=(b // nb,),
        in_specs=[
            pl.BlockSpec((nb, ts, d), lambda i: (i, 0, 0)),
            pl.BlockSpec((nb, D_KV, WINDOW), lambda i: (i, 0, 0)),
            pl.BlockSpec((nb, D_KV, WINDOW), lambda i: (i, 0, 0)),
            const((1, d)),
            const((1, d)),
            const((d, 2 * D_KV)),
            const((d, D_ATTN)),
            const((d, D_ATTN)),
            const((1, D_KV)),
            const((1, D_KV)),
            const((D_KV, D_KV)),
            const((rows, WINDOW)),
            const((rows, NEW_KEY_ROWS)),
            const((rows, 1)),
            const((D_ATTN, d)),
        ],
        out_specs=[
            pl.BlockSpec((nb, ts, d), lambda i: (i, 0, 0)),
            pl.BlockSpec((nb, D_KV, WINDOW), lambda i: (i, 0, 0)),
            pl.BlockSpec((nb, D_KV, WINDOW), lambda i: (i, 0, 0)),
        ],
        out_shape=[
            jax.ShapeDtypeStruct((b, ts, d), F32),
            jax.ShapeDtypeStruct((b, D_KV, WINDOW), F32),
            jax.ShapeDtypeStruct((b, D_KV, WINDOW), F32),
        ],
        scratch_shapes=[
            pltpu.VMEM((nb, ts, D_KV), F32),
            pltpu.VMEM((nb, ts, D_KV), F32),
            pltpu.VMEM((GROUP, nb, ts, D_KV), F32),
            pltpu.VMEM((nb, ts, D_ATTN), F32),
        ],
        compiler_params=pltpu.CompilerParams(
            dimension_semantics=("arbitrary",), vmem_limit_bytes=VMEM_LIMIT),
        name="mixer_b_sample",
    )(x, ck_t, cv_t, gkv, gb, wkv, wq, wz, kg, qg, seg, bias_c, bias_n, sink, wout)


def _alibi_slopes():
    return 2.0 ** (-8.0 * np.arange(1, N_HEADS + 1) / N_HEADS)


def _band_bias(n_query, n_key, first_block):
    dist = np.arange(n_query)[:, None] + WINDOW - np.arange(n_key)[None, :]
    allowed = (dist >= 0) & (dist < WINDOW)
    if first_block:
        allowed = allowed & (np.arange(n_key)[None, :] >= WINDOW)
    bias = -_alibi_slopes()[:, None, None] * dist[None].astype(np.float64) * LOG2E
    return np.where(allowed[None], bias, -np.inf).astype(np.float32)


def _group_major(w, axis):
    shape = w.shape
    w = w.reshape(shape[:axis] + (N_KV, GROUP, HEAD_DIM) + shape[axis + 1:])
    w = jnp.swapaxes(w, axis, axis + 1)
    return w.reshape(shape)


def kernel(x_prompt, x_sample, state_conv, cache_k, cache_v, norm_a, w_in_a, w_dw_a, b_dw_a, ln_g_a, ln_b_a, w_out_a, norm_kv, w_kv, k_norm, norm_b, w_in_b, q_norm, sinks_b, w_out_b):
    assert norm_a.shape[0] == 1 and norm_b.shape[0] == 1
    row = lambda v: v.reshape(1, -1).astype(F32)

    a_args = (row(norm_a[0]), w_in_a[0].astype(BF16), w_dw_a[0].astype(F32), row(b_dw_a[0]),
              row(ln_g_a[0]), row(ln_b_a[0]), w_out_a[0].astype(BF16))
    xp, st_p = _mixer_a_prompt(x_prompt, *a_args)
    xs, st_s = _mixer_a_sample(x_sample, state_conv, *a_args)

    wq = _group_major(w_in_b[0][:, :D_ATTN], 1).astype(BF16)
    wz = _group_major(w_in_b[0][:, D_ATTN:], 1).astype(BF16)
    wout = _group_major(w_out_b[0], 0).astype(BF16)
    sinks = sinks_b[0].astype(F32) * LOG2E
    kg = row(jnp.tile(k_norm, N_KV))
    qg = row(jnp.tile(q_norm[0], N_KV)) * (HEAD_DIM ** -0.5 * LOG2E)
    seg = jnp.asarray(np.kron(np.eye(N_KV), np.full((HEAD_DIM, HEAD_DIM), 1.0 / HEAD_DIM)), BF16)
    shared = (row(norm_kv), row(norm_b[0]), w_kv.astype(BF16), wq, wz, kg, qg, seg)

    span = WINDOW + Q_BLOCK
    bias_p = jnp.asarray(np.stack([_band_bias(Q_BLOCK, span, True), _band_bias(Q_BLOCK, span, False)])
                         .reshape(2, N_KV, GROUP * Q_BLOCK, span))
    yp, k_tail, v_tail = _mixer_b_prompt(xp, sinks, *shared, bias_p, wout)

    ts = x_sample.shape[1]
    head_of_row = np.array([kvh * GROUP + g for g in range(GROUP) for kvh in range(N_KV)])
    bias_s = _band_bias(ts, WINDOW + NEW_KEY_ROWS, False)[head_of_row].reshape(N_HEADS * ts, WINDOW + NEW_KEY_ROWS)
    sink_s = jnp.repeat(sinks[head_of_row], ts).reshape(N_HEADS * ts, 1)
    nb = cache_k.shape[0]
    to_cols = lambda c: jnp.transpose(c, (0, 2, 3, 1)).reshape(nb, D_KV, WINDOW)
    from_cols = lambda c: jnp.transpose(c.reshape(nb, N_KV, HEAD_DIM, WINDOW), (0, 3, 1, 2))
    ys, nk, nv = _mixer_b_sample(xs, to_cols(cache_k), to_cols(cache_v), *shared,
                                 jnp.asarray(bias_s[:, :WINDOW]), jnp.asarray(bias_s[:, WINDOW:]), sink_s, wout)

    kv4 = lambda a: a.reshape(a.shape[0], WINDOW, N_KV, HEAD_DIM)
    return (yp, ys, st_p[None, :, PREFIX_SKIP:, :], st_s, kv4(k_tail), kv4(v_tail), from_cols(nk), from_cols(nv))
```

```python
import numpy as np
import jax
import jax.numpy as jnp
from jax import lax
from jax.experimental import pallas as pl
from jax.experimental.pallas import tpu as pltpu

F32 = jnp.float32
BF16 = jnp.bfloat16

D_MODEL = 1024
D_CONV = 1024
CONV_WIDTH = 31
CONV_STATE = CONV_WIDTH - 1
N_HEADS = 16
N_KV = 4
GROUP = N_HEADS // N_KV
HEAD_DIM = 64
D_ATTN = N_HEADS * HEAD_DIM
D_KV = N_KV * HEAD_DIM
WINDOW = 128
Q_BLOCK = 128
RMS_EPS = 1e-6
LN_EPS = 1e-5
LOG2E = float(np.log2(np.e))

SUBLANES = 8
LANES = 128
LANE_CHUNKS = D_CONV // LANES
assert LANE_CHUNKS == SUBLANES
BF16_ROWS = 16
PAIR_ROWS = 2 * LANE_CHUNKS
assert PAIR_ROWS == BF16_ROWS
PREFIX_ROWS = 32
PREFIX_SKIP = PREFIX_ROWS - CONV_STATE
TILE_M = 1024
CONV_ROWS = 32
SAMPLE_BLOCK = 32
SAMPLE_BLOCK_ATTN = 16
NEW_KEY_ROWS = 16
SEQ_UNROLL = 4
VMEM_LIMIT = 56 * 1024 * 1024


def _dot(a, b):
    return jnp.dot(a, b, preferred_element_type=F32)


def _dot_nt(a, b):
    return lax.dot_general(a, b, (((1,), (1,)), ((), ())), preferred_element_type=F32)


def _rms_unit(x):
    return x * lax.rsqrt(jnp.mean(x * x, axis=-1, keepdims=True) + RMS_EPS)


def _silu(x):
    return x * jax.nn.sigmoid(x)


def _head_rms(x, seg_mean, gain):
    ms = _dot((x * x).astype(BF16), seg_mean)
    return x * lax.rsqrt(ms + RMS_EPS) * gain


def _mixer_a_prompt_kernel(x_ref, g_ref, win_ref, w16_ref, bdw_ref, lng_ref, lnb_ref, wout_ref,
                           y_ref, st_ref, hist_ref, hprev_ref, even_ref, odd_ref, tmp_ref, zg_ref, act_ref):
    t = pl.program_id(1)
    tm = x_ref.shape[1]
    prefix = PREFIX_ROWS * LANE_CHUNKS
    total = prefix + tm * LANE_CHUNKS

    hist_ref[0:prefix, :] = jnp.where(t > 0, hprev_ref[...], jnp.zeros((prefix, LANES), F32))

    x = x_ref[0]
    h = (_rms_unit(x) * g_ref[...]).astype(BF16)
    a = _dot(h, win_ref[:, 0:D_CONV])
    gl = _dot(h, win_ref[:, D_CONV:2 * D_CONV])
    v = a * jax.nn.sigmoid(gl)
    for c in range(LANE_CHUNKS):
        hist_ref[pl.ds(prefix + c, tm, stride=LANE_CHUNKS), :] = v[:, c * LANES:(c + 1) * LANES]
    zg_ref[...] = _silu(_dot(h, win_ref[:, 2 * D_CONV:3 * D_CONV]))

    @pl.when(t == pl.num_programs(1) - 1)
    def _():
        st_ref[0] = v[tm - PREFIX_ROWS:tm, :]

    even_ref[...] = hist_ref[...].astype(BF16)
    odd_ref[...] = hist_ref[LANE_CHUNKS:total - LANE_CHUNKS, :].astype(BF16)

    lng = lng_ref[...]
    lnb = lnb_ref[...]
    rows = CONV_ROWS * LANE_CHUNKS
    for i in range(tm // CONV_ROWS):
        r0 = i * CONV_ROWS
        acc = None
        for k in range(CONV_WIDTH):
            tau = r0 + PREFIX_SKIP + k
            src, row = (even_ref, tau * LANE_CHUNKS) if tau % 2 == 0 else (odd_ref, (tau - 1) * LANE_CHUNKS)
            tap = src[row:row + rows, :].astype(F32).reshape(CONV_ROWS // 2, PAIR_ROWS, LANES)
            term = tap * w16_ref[k * PAIR_ROWS:(k + 1) * PAIR_ROWS, :].astype(F32)
            acc = term if acc is None else acc + term
        acc = acc.reshape(CONV_ROWS, LANE_CHUNKS, LANES) + bdw_ref[...]
        tmp_ref[i * rows:(i + 1) * rows, :] = acc.reshape(rows, LANES)
        conv = jnp.concatenate(
            [tmp_ref[pl.ds(i * rows + c, CONV_ROWS, stride=LANE_CHUNKS), :] for c in range(LANE_CHUNKS)], axis=1)
        mu = jnp.mean(conv, axis=-1, keepdims=True)
        xc = conv - mu
        var = jnp.mean(xc * xc, axis=-1, keepdims=True)
        y = xc * lax.rsqrt(var + LN_EPS) * lng + lnb
        act_ref[r0:r0 + CONV_ROWS, :] = (_silu(y) * zg_ref[r0:r0 + CONV_ROWS, :]).astype(BF16)

    y_ref[0] = x + _dot(act_ref[...], wout_ref[...])

    hprev_ref[...] = hist_ref[tm * LANE_CHUNKS:total, :]


def _mixer_a_prompt(x, g, w_in, w_dw, b_dw, ln_g, ln_b, w_out):
    b, t, d = x.shape
    tm = TILE_M
    const = lambda shape: pl.BlockSpec(shape, lambda i, j: (0,) * len(shape))
    w_pairs = jnp.tile(w_dw.reshape(CONV_WIDTH, 1, LANE_CHUNKS, LANES), (1, 2, 1, 1))
    w_pairs = w_pairs.reshape(CONV_WIDTH * PAIR_ROWS, LANES).astype(BF16)
    return pl.pallas_call(
        _mixer_a_prompt_kernel,
        grid=(b, t // tm),
        in_specs=[
            pl.BlockSpec((1, tm, d), lambda i, j: (i, j, 0)),
            const((1, d)),
            const((d, 3 * D_CONV)),
            const((CONV_WIDTH * PAIR_ROWS, LANES)),
            const((LANE_CHUNKS, LANES)),
            const((1, D_CONV)),
            const((1, D_CONV)),
            const((D_CONV, d)),
        ],
        out_specs=[
            pl.BlockSpec((1, tm, d), lambda i, j: (i, j, 0)),
            pl.BlockSpec((1, PREFIX_ROWS, D_CONV), lambda i, j: (i, 0, 0)),
        ],
        out_shape=[
            jax.ShapeDtypeStruct((b, t, d), F32),
            jax.ShapeDtypeStruct((b, PREFIX_ROWS, D_CONV), F32),
        ],
        scratch_shapes=[
            pltpu.VMEM(((PREFIX_ROWS + tm) * LANE_CHUNKS, LANES), F32),
            pltpu.VMEM((PREFIX_ROWS * LANE_CHUNKS, LANES), F32),
            pltpu.VMEM(((PREFIX_ROWS + tm) * LANE_CHUNKS, LANES), BF16),
            pltpu.VMEM(((PREFIX_ROWS + tm - 2) * LANE_CHUNKS, LANES), BF16),
            pltpu.VMEM((tm * LANE_CHUNKS, LANES), F32),
            pltpu.VMEM((tm, D_CONV), F32),
            pltpu.VMEM((tm, D_CONV), BF16),
        ],
        compiler_params=pltpu.CompilerParams(
            dimension_semantics=("arbitrary", "arbitrary"), vmem_limit_bytes=VMEM_LIMIT),
        name="mixer_a_prompt",
    )(x, g, w_in, w_pairs, b_dw.reshape(LANE_CHUNKS, LANES), ln_g, ln_b, w_out)


def _mixer_a_sample_kernel(x_ref, st_ref, g_ref, win_ref, wdw_ref, bdw_ref, lng_ref, lnb_ref, wout_ref,
                           y_ref, nst_ref, vf_ref, slab_ref, back_ref, zg_ref):
    nb, ts, d = x_ref.shape
    x = x_ref[...].reshape(nb * ts, d)
    h = (_rms_unit(x) * g_ref[...]).astype(BF16)
    a = _dot(h, win_ref[:, 0:D_CONV])
    gl = _dot(h, win_ref[:, D_CONV:2 * D_CONV])
    v = a * jax.nn.sigmoid(gl)
    zg_ref[...] = _silu(_dot(h, win_ref[:, 2 * D_CONV:3 * D_CONV]))

    for c in range(LANE_CHUNKS):
        slab_ref[c] = v[:, c * LANES:(c + 1) * LANES]
    vf_ref[0:CONV_STATE] = st_ref[...]
    for t in range(ts):
        for c in range(LANE_CHUNKS):
            vf_ref[CONV_STATE + t, :, c * LANES:(c + 1) * LANES] = slab_ref[c, pl.ds(t, nb, stride=ts), :]
    nst_ref[...] = vf_ref[ts:ts + CONV_STATE]

    bdw = bdw_ref[...]
    lng = lng_ref[...]
    lnb = lnb_ref[...]
    for b0 in range(0, nb, SUBLANES):
        acc = None
        for k in range(CONV_WIDTH):
            term = vf_ref[k:k + ts, b0:b0 + SUBLANES, :] * wdw_ref[k:k + 1, :]
            acc = term + bdw if acc is None else acc + term
        conv = acc.reshape(ts * SUBLANES, D_CONV)
        mu = jnp.mean(conv, axis=-1, keepdims=True)
        xc = conv - mu
        var = jnp.mean(xc * xc, axis=-1, keepdims=True)
        sy = _silu(xc * lax.rsqrt(var + LN_EPS) * lng + lnb)
        for t in range(ts):
            for c in range(LANE_CHUNKS):
                back_ref[c, pl.ds(b0 * ts + t, SUBLANES, stride=ts), :] = (
                    sy[t * SUBLANES:(t + 1) * SUBLANES, c * LANES:(c + 1) * LANES])

    act = jnp.concatenate([back_ref[c] for c in range(LANE_CHUNKS)], axis=1) * zg_ref[...]
    y_ref[...] = (x + _dot(act.astype(BF16), wout_ref[...])).reshape(nb, ts, d)


def _mixer_a_sample(x, state_t, g, w_in, w_dw, b_dw, ln_g, ln_b, w_out):
    b, ts, d = x.shape
    nb = SAMPLE_BLOCK
    const = lambda shape: pl.BlockSpec(shape, lambda i: (0,) * len(shape))
    return pl.pallas_call(
        _mixer_a_sample_kernel,
        grid=(b // nb,),
        in_specs=[
            pl.BlockSpec((nb, ts, d), lambda i: (i, 0, 0)),
            pl.BlockSpec((CONV_STATE, nb, D_CONV), lambda i: (0, i, 0)),
            const((1, d)),
            const((d, 3 * D_CONV)),
            const((CONV_WIDTH, D_CONV)),
            const((1, D_CONV)),
            const((1, D_CONV)),
            const((1, D_CONV)),
            const((D_CONV, d)),
        ],
        out_specs=[
            pl.BlockSpec((nb, ts, d), lambda i: (i, 0, 0)),
            pl.BlockSpec((CONV_STATE, nb, D_CONV), lambda i: (0, i, 0)),
        ],
        out_shape=[
            jax.ShapeDtypeStruct((b, ts, d), F32),
            jax.ShapeDtypeStruct((CONV_STATE, b, D_CONV), F32),
        ],
        scratch_shapes=[
            pltpu.VMEM((CONV_STATE + ts, nb, D_CONV), F32),
            pltpu.VMEM((LANE_CHUNKS, nb * ts, LANES), F32),
            pltpu.VMEM((LANE_CHUNKS, nb * ts, LANES), F32),
            pltpu.VMEM((nb * ts, D_CONV), F32),
        ],
        compiler_params=pltpu.CompilerParams(
            dimension_semantics=("arbitrary",), vmem_limit_bytes=VMEM_LIMIT),
        name="mixer_a_sample",
    )(x, state_t, g, w_in, w_dw, b_dw, ln_g, ln_b, w_out)


def _project_kvqz(x, gkv_ref, gb_ref, wkv_ref, wq_ref, wz_ref, kg_ref, qg_ref, seg_ref):
    xn = _rms_unit(x)
    hk = (xn * gkv_ref[...]).astype(BF16)
    hq = (xn * gb_ref[...]).astype(BF16)
    seg = seg_ref[...]
    kv = _dot(hk, wkv_ref[...])
    k = _head_rms(kv[:, 0:D_KV], seg, kg_ref[...])
    v = kv[:, D_KV:2 * D_KV]
    q_all = _dot(hq, wq_ref[...])
    q = []
    for g in range(GROUP):
        cols = slice(g * D_KV, (g + 1) * D_KV)
        q.append(_head_rms(q_all[:, cols], seg, qg_ref[...]))
    zg = _silu(_dot(hq, wz_ref[...]))
    return k, v, q, zg


def _lane_segment_mask(kvh):
    lane = lax.broadcasted_iota(jnp.int32, (1, D_KV), 1)
    return (lane >= kvh * HEAD_DIM) & (lane < (kvh + 1) * HEAD_DIM)


def _softmax_unnorm(s, sink):
    m = jnp.maximum(jnp.max(s, axis=-1, keepdims=True), sink)
    p = jnp.exp2(s - m)
    denom = jnp.sum(p, axis=-1, keepdims=True) + jnp.exp2(sink - m)
    return p, denom


def _mixer_b_prompt_kernel(sink_ref, x_ref, gkv_ref, gb_ref, wkv_ref, wq_ref, wz_ref, kg_ref, qg_ref,
                           seg_ref, bias_ref, wout_ref,
                           y_ref, kt_ref, vt_ref,
                           km_ref, vm_ref, kprev_ref, vprev_ref, q_ref, o_ref, hq_ref, zg_ref):
    t = pl.program_id(1)
    tm = x_ref.shape[1]

    zeros = jnp.zeros((N_KV, WINDOW, D_KV), BF16)
    km_ref[:, 0:WINDOW, :] = jnp.where(t > 0, kprev_ref[...], zeros)
    vm_ref[:, 0:WINDOW, :] = jnp.where(t > 0, vprev_ref[...], zeros)

    x = x_ref[0]
    xn = _rms_unit(x)
    hk = (xn * gkv_ref[...]).astype(BF16)
    hq_ref[...] = (xn * gb_ref[...]).astype(BF16)
    seg = seg_ref[...]
    kv = _dot(hk, wkv_ref[...])
    k = _head_rms(kv[:, 0:D_KV], seg, kg_ref[...])
    v = kv[:, D_KV:2 * D_KV]
    for kvh in range(N_KV):
        mask = _lane_segment_mask(kvh)
        km_ref[kvh, WINDOW:WINDOW + tm, :] = jnp.where(mask, k, 0.0).astype(BF16)
        vm_ref[kvh, WINDOW:WINDOW + tm, :] = jnp.where(mask, v, 0.0).astype(BF16)

    @pl.when(t == pl.num_programs(1) - 1)
    def _():
        kt_ref[0] = k[tm - WINDOW:tm, :]
        vt_ref[0] = v[tm - WINDOW:tm, :]

    span = WINDOW + Q_BLOCK

    def project_block(j):
        r0 = j * Q_BLOCK
        hq = hq_ref[r0:r0 + Q_BLOCK, :]
        q_all = _dot(hq, wq_ref[...])
        for g in range(GROUP):
            cols = slice(g * D_KV, (g + 1) * D_KV)
            q_ref[g, r0:r0 + Q_BLOCK, :] = _head_rms(q_all[:, cols], seg, qg_ref[...]).astype(BF16)
        zg_ref[r0:r0 + Q_BLOCK, :] = _silu(_dot(hq, wz_ref[...]))

    def attend_block(j):
        r0 = j * Q_BLOCK
        bsel = jnp.where(t == 0, 0, 1) if j == 0 else 1
        qstack = jnp.concatenate([q_ref[g, r0:r0 + Q_BLOCK, :] for g in range(GROUP)], axis=0)
        out = None
        keys = jnp.concatenate([km_ref[kvh, r0:r0 + span, :] for kvh in range(N_KV)], axis=0)
        s_all = _dot_nt(qstack, keys)
        for kvh in range(N_KV):
            s = s_all[:, kvh * span:(kvh + 1) * span] + bias_ref[bsel, kvh]
            probs = []
            for g in range(GROUP):
                p, denom = _softmax_unnorm(s[g * Q_BLOCK:(g + 1) * Q_BLOCK, :], sink_ref[kvh * GROUP + g])
                probs.append((p * (1.0 / denom)).astype(BF16))
            o = _dot(jnp.concatenate(probs, axis=0), vm_ref[kvh, r0:r0 + span, :])
            out = o if out is None else out + o
        for g in range(GROUP):
            o_ref[r0:r0 + Q_BLOCK, g * D_KV:(g + 1) * D_KV] = out[g * Q_BLOCK:(g + 1) * Q_BLOCK, :]

    def output_block(j):
        rows = slice(j * Q_BLOCK, (j + 1) * Q_BLOCK)
        gated = (o_ref[rows, :] * zg_ref[rows, :]).astype(BF16)
        y_ref[0, rows, :] = x_ref[0, rows, :] + _dot(gated, wout_ref[...])

    n_blocks = tm // Q_BLOCK
    project_block(0)
    for j in range(n_blocks):
        if j + 1 < n_blocks:
            project_block(j + 1)
        attend_block(j)
        if j > 0:
            output_block(j - 1)
    output_block(n_blocks - 1)

    for kvh in range(N_KV):
        mask = _lane_segment_mask(kvh)
        kprev_ref[kvh] = jnp.where(mask, k[tm - WINDOW:tm, :], 0.0).astype(BF16)
        vprev_ref[kvh] = jnp.where(mask, v[tm - WINDOW:tm, :], 0.0).astype(BF16)


def _mixer_b_prompt(x, sinks, gkv, gb, wkv, wq, wz, kg, qg, seg, bias, wout):
    b, t, d = x.shape
    tm = TILE_M
    const = lambda shape: pl.BlockSpec(shape, lambda i, j: (0,) * len(shape), pipeline_mode=pl.Buffered(1))
    return pl.pallas_call(
        _mixer_b_prompt_kernel,
        grid=(b, t // tm),
        in_specs=[
            pl.BlockSpec(memory_space=pltpu.SMEM),
            pl.BlockSpec((1, tm, d), lambda i, j: (i, j, 0)),
            const((1, d)),
            const((1, d)),
            const((d, 2 * D_KV)),
            const((d, D_ATTN)),
            const((d, D_ATTN)),
            const((1, D_KV)),
            const((1, D_KV)),
            const((D_KV, D_KV)),
            const((2, N_KV, GROUP * Q_BLOCK, WINDOW + Q_BLOCK)),
            const((D_ATTN, d)),
        ],
        out_specs=[
            pl.BlockSpec((1, tm, d), lambda i, j: (i, j, 0)),
            pl.BlockSpec((1, WINDOW, D_KV), lambda i, j: (i, 0, 0)),
            pl.BlockSpec((1, WINDOW, D_KV), lambda i, j: (i, 0, 0)),
        ],
        out_shape=[
            jax.ShapeDtypeStruct((b, t, d), F32),
            jax.ShapeDtypeStruct((b, WINDOW, D_KV), F32),
            jax.ShapeDtypeStruct((b, WINDOW, D_KV), F32),
        ],
        scratch_shapes=[
            pltpu.VMEM((N_KV, WINDOW + tm, D_KV), BF16),
            pltpu.VMEM((N_KV, WINDOW + tm, D_KV), BF16),
            pltpu.VMEM((N_KV, WINDOW, D_KV), BF16),
            pltpu.VMEM((N_KV, WINDOW, D_KV), BF16),
            pltpu.VMEM((GROUP, tm, D_KV), BF16),
            pltpu.VMEM((tm, D_ATTN), F32),
            pltpu.VMEM((tm, d), BF16),
            pltpu.VMEM((tm, D_ATTN), F32),
        ],
        compiler_params=pltpu.CompilerParams(
            dimension_semantics=("arbitrary", "arbitrary"), vmem_limit_bytes=VMEM_LIMIT),
        name="mixer_b_prompt",
    )(sinks, x, gkv, gb, wkv, wq, wz, kg, qg, seg, bias, wout)


def _mixer_b_sample_kernel(x_ref, ck_ref, cv_ref, gkv_ref, gb_ref, wkv_ref, wq_ref, wz_ref, kg_ref,
                           qg_ref, seg_ref, biasc_ref, biasn_ref, sink_ref, wout_ref,
                           y_ref, nk_ref, nv_ref,
                           kn_ref, vn_ref, q_ref, o_ref):
    nb, ts, d = x_ref.shape
    x = x_ref[...].reshape(nb * ts, d)
    k, v, q, zg = _project_kvqz(x, gkv_ref, gb_ref, wkv_ref, wq_ref, wz_ref, kg_ref, qg_ref, seg_ref)
    for g in range(GROUP):
        q_ref[g] = q[g].reshape(nb, ts, D_KV)
    kn_ref[...] = k.reshape(nb, ts, D_KV)
    vn_ref[...] = v.reshape(nb, ts, D_KV)

    masks = [_lane_segment_mask(kvh) for kvh in range(N_KV)]
    bias_c = biasc_ref[...]
    bias_n = biasn_ref[...]
    sink = sink_ref[...]
    new_lanes = lax.broadcasted_iota(jnp.int32, (1, WINDOW), 1) >= WINDOW - ts
    pad_rows = jnp.zeros((NEW_KEY_ROWS - ts, D_KV), F32)
    head_rows = jnp.zeros((WINDOW - ts, D_KV), F32)

    def seq(i, carry):
        k_old = ck_ref[i]
        v_old = cv_ref[i]
        k_new = kn_ref[i]
        v_new = vn_ref[i]
        rows = []
        for g in range(GROUP):
            qg_rows = q_ref[g, i]
            for kvh in range(N_KV):
                rows.append(jnp.where(masks[kvh], qg_rows, jnp.zeros_like(qg_rows)))
        lhs = jnp.concatenate(rows, axis=0).astype(BF16)
        k_new16 = jnp.concatenate([k_new, pad_rows], axis=0).astype(BF16)
        v_new16 = jnp.concatenate([v_new, pad_rows], axis=0).astype(BF16)
        s_c = _dot(lhs, k_old.astype(BF16)) + bias_c
        s_n = _dot_nt(lhs, k_new16) + bias_n
        m = jnp.maximum(jnp.maximum(jnp.max(s_c, axis=-1, keepdims=True), jnp.max(s_n, axis=-1, keepdims=True)),
                        sink)
        p_c = jnp.exp2(s_c - m)
        p_n = jnp.exp2(s_n - m)
        denom = (jnp.sum(p_c, axis=-1, keepdims=True) + jnp.sum(p_n, axis=-1, keepdims=True)
                 + jnp.exp2(sink - m))
        r = 1.0 / denom
        o = _dot_nt((p_c * r).astype(BF16), v_old.astype(BF16)) + _dot((p_n * r).astype(BF16), v_new16)
        for g in range(GROUP):
            acc = None
            for kvh in range(N_KV):
                r0 = (g * N_KV + kvh) * ts
                part = jnp.where(masks[kvh], o[r0:r0 + ts, :], 0.0)
                acc = part if acc is None else acc + part
            o_ref[i, :, g * D_KV:(g + 1) * D_KV] = acc
        k_cols = jnp.concatenate([head_rows, k_new], axis=0).T
        v_cols = jnp.concatenate([head_rows, v_new], axis=0).T
        nk_ref[i] = jnp.where(new_lanes, k_cols, pltpu.roll(k_old, WINDOW - ts, axis=1))
        nv_ref[i] = jnp.where(new_lanes, v_cols, pltpu.roll(v_old, WINDOW - ts, axis=1))
        return carry

    lax.fori_loop(0, nb, seq, 0, unroll=SEQ_UNROLL)

    og = (o_ref[...].reshape(nb * ts, D_ATTN) * zg).astype(BF16)
    y_ref[...] = (x + _dot(og, wout_ref[...])).reshape(nb, ts, d)


def _mixer_b_sample(x, ck_t, cv_t, gkv, gb, wkv, wq, wz, kg, qg, seg, bias_c, bias_n, sink, wout):
    b, ts, d = x.shape
    nb = SAMPLE_BLOCK_ATTN
    const = lambda shape: pl.BlockSpec(shape, lambda i: (0,) * len(shape))
    rows = N_HEADS * ts
    return pl.pallas_call(
        _mixer_b_sample_kernel,
        grid=(b // nb,),
        in_specs=[
            pl.BlockSpec((nb, ts, d), lambda i: (i, 0, 0)),
            pl.BlockSpec((nb, D_KV, WINDOW), lambda i: (i, 0, 0)),
            pl.BlockSpec((nb, D_KV, WINDOW), lambda i: (i, 0, 0)),
            const((1, d)),
            const((1, d)),
            const((d, 2 * D_KV)),
            const((d, D_ATTN)),
            const((d, D_ATTN)),
            const((1, D_KV)),
            const((1, D_KV)),
            const((D_KV, D_KV)),
            const((rows, WINDOW)),
            const((rows, NEW_KEY_ROWS)),
            const((rows, 1)),
            const((D_ATTN, d)),
        ],
        out_specs=[
            pl.BlockSpec((nb, ts, d), lambda i: (i, 0, 0)),
            pl.BlockSpec((nb, D_KV, WINDOW), lambda i: (i, 0, 0)),
            pl.BlockSpec((nb, D_KV, WINDOW), lambda i: (i, 0, 0)),
        ],
        out_shape=[
            jax.ShapeDtypeStruct((b, ts, d), F32),
            jax.ShapeDtypeStruct((b, D_KV, WINDOW), F32),
            jax.ShapeDtypeStruct((b, D_KV, WINDOW), F32),
        ],
        scratch_shapes=[
            pltpu.VMEM((nb, ts, D_KV), F32),
            pltpu.VMEM((nb, ts, D_KV), F32),
            pltpu.VMEM((GROUP, nb, ts, D_KV), F32),
            pltpu.VMEM((nb, ts, D_ATTN), F32),
        ],
        compiler_params=pltpu.CompilerParams(
            dimension_semantics=("arbitrary",), vmem_limit_bytes=VMEM_LIMIT),
        name="mixer_b_sample",
    )(x, ck_t, cv_t, gkv, gb, wkv, wq, wz, kg, qg, seg, bias_c, bias_n, sink, wout)


def _alibi_slopes():
    return 2.0 ** (-8.0 * np.arange(1, N_HEADS + 1) / N_HEADS)


def _band_bias(n_query, n_key, first_block):
    dist = np.arange(n_query)[:, None] + WINDOW - np.arange(n_key)[None, :]
    allowed = (dist >= 0) & (dist < WINDOW)
    if first_block:
        allowed = allowed & (np.arange(n_key)[None, :] >= WINDOW)
    bias = -_alibi_slopes()[:, None, None] * dist[None].astype(np.float64) * LOG2E
    return np.where(allowed[None], bias, -np.inf).astype(np.float32)


def _group_major(w, axis):
    shape = w.shape
    w = w.reshape(shape[:axis] + (N_KV, GROUP, HEAD_DIM) + shape[axis + 1:])
    w = jnp.swapaxes(w, axis, axis + 1)
    return w.reshape(shape)


def kernel(x_prompt, x_sample, state_conv, cache_k, cache_v, norm_a, w_in_a, w_dw_a, b_dw_a, ln_g_a, ln_b_a, w_out_a, norm_kv, w_kv, k_norm, norm_b, w_in_b, q_norm, sinks_b, w_out_b):
    assert norm_a.shape[0] == 1 and norm_b.shape[0] == 1
    row = lambda v: v.reshape(1, -1).astype(F32)

    a_args = (row(norm_a[0]), w_in_a[0].astype(BF16), w_dw_a[0].astype(F32), row(b_dw_a[0]),
              row(ln_g_a[0]), row(ln_b_a[0]), w_out_a[0].astype(BF16))
    xp, st_p = _mixer_a_prompt(x_prompt, *a_args)
    xs, st_s = _mixer_a_sample(x_sample, jnp.transpose(state_conv[0], (1, 0, 2)), *a_args)
    st_s = jnp.transpose(st_s, (1, 0, 2))[None]

    wq = _group_major(w_in_b[0][:, :D_ATTN], 1).astype(BF16)
    wz = _group_major(w_in_b[0][:, D_ATTN:], 1).astype(BF16)
    wout = _group_major(w_out_b[0], 0).astype(BF16)
    sinks = sinks_b[0].astype(F32) * LOG2E
    kg = row(jnp.tile(k_norm, N_KV))
    qg = row(jnp.tile(q_norm[0], N_KV)) * (HEAD_DIM ** -0.5 * LOG2E)
    seg = jnp.asarray(np.kron(np.eye(N_KV), np.full((HEAD_DIM, HEAD_DIM), 1.0 / HEAD_DIM)), BF16)
    shared = (row(norm_kv), row(norm_b[0]), w_kv.astype(BF16), wq, wz, kg, qg, seg)

    span = WINDOW + Q_BLOCK
    bias_p = jnp.asarray(np.stack([_band_bias(Q_BLOCK, span, True), _band_bias(Q_BLOCK, span, False)])
                         .reshape(2, N_KV, GROUP * Q_BLOCK, span))
    yp, k_tail, v_tail = _mixer_b_prompt(xp, sinks, *shared, bias_p, wout)

    ts = x_sample.shape[1]
    head_of_row = np.array([kvh * GROUP + g for g in range(GROUP) for kvh in range(N_KV)])
    bias_s = _band_bias(ts, WINDOW + NEW_KEY_ROWS, False)[head_of_row].reshape(N_HEADS * ts, WINDOW + NEW_KEY_ROWS)
    sink_s = jnp.repeat(sinks[head_of_row], ts).reshape(N_HEADS * ts, 1)
    nb = cache_k.shape[0]
    to_cols = lambda c: jnp.transpose(c, (0, 2, 3, 1)).reshape(nb, D_KV, WINDOW)
    from_cols = lambda c: jnp.transpose(c.reshape(nb, N_KV, HEAD_DIM, WINDOW), (0, 3, 1, 2))
    ys, nk, nv = _mixer_b_sample(xs, to_cols(cache_k), to_cols(cache_v), *shared,
                                 jnp.asarray(bias_s[:, :WINDOW]), jnp.asarray(bias_s[:, WINDOW:]), sink_s, wout)

    kv4 = lambda a: a.reshape(a.shape[0], WINDOW, N_KV, HEAD_DIM)
    return (yp, ys, st_p[None, :, PREFIX_SKIP:, :], st_s, kv4(k_tail), kv4(v_tail), from_cols(nk), from_cols(nv))
```

```python
import numpy as np
import jax
import jax.numpy as jnp
from jax import lax
from jax.experimental import pallas as pl
from jax.experimental.pallas import tpu as pltpu

F32 = jnp.float32
BF16 = jnp.bfloat16

D_MODEL = 1024
D_CONV = 1024
CONV_WIDTH = 31
CONV_STATE = CONV_WIDTH - 1
N_HEADS = 16
N_KV = 4
GROUP = N_HEADS // N_KV
HEAD_DIM = 64
D_ATTN = N_HEADS * HEAD_DIM
D_KV = N_KV * HEAD_DIM
WINDOW = 128
Q_BLOCK = 128
RMS_EPS = 1e-6
LN_EPS = 1e-5
LOG2E = float(np.log2(np.e))

SUBLANES = 8
LANES = 128
LANE_CHUNKS = D_CONV // LANES
assert LANE_CHUNKS == SUBLANES
BF16_ROWS = 16
PAIR_ROWS = 2 * LANE_CHUNKS
assert PAIR_ROWS == BF16_ROWS
PREFIX_ROWS = 32
PREFIX_SKIP = PREFIX_ROWS - CONV_STATE
TILE_M = 1024
CONV_ROWS = 32
SAMPLE_BLOCK = 32
SAMPLE_BLOCK_ATTN = 16
NEW_KEY_ROWS = 16
PROJ_ROWS = 256
SEQ_UNROLL = 8
VMEM_LIMIT = 56 * 1024 * 1024


def _dot(a, b):
    return jnp.dot(a, b, preferred_element_type=F32)


def _dot_nt(a, b):
    return lax.dot_general(a, b, (((1,), (1,)), ((), ())), preferred_element_type=F32)


def _rms_unit(x):
    return x * lax.rsqrt(jnp.mean(x * x, axis=-1, keepdims=True) + RMS_EPS)


def _silu(x):
    return x * jax.nn.sigmoid(x)


def _head_rms(x, seg_mean, gain):
    ms = _dot((x * x).astype(BF16), seg_mean)
    return x * lax.rsqrt(ms + RMS_EPS) * gain


def _mixer_a_prompt_kernel(x_ref, g_ref, win_ref, w16_ref, bdw_ref, lng_ref, lnb_ref, wout_ref,
                           y_ref, st_ref, hist_ref, hprev_ref, even_ref, odd_ref, tmp_ref, zg_ref, act_ref):
    t = pl.program_id(1)
    tm = x_ref.shape[1]
    prefix = PREFIX_ROWS * LANE_CHUNKS
    total = prefix + tm * LANE_CHUNKS

    hist_ref[0:prefix, :] = jnp.where(t > 0, hprev_ref[...], jnp.zeros((prefix, LANES), F32))

    x = x_ref[0]
    h = (_rms_unit(x) * g_ref[...]).astype(BF16)
    a = _dot(h, win_ref[:, 0:D_CONV])
    gl = _dot(h, win_ref[:, D_CONV:2 * D_CONV])
    v = a * jax.nn.sigmoid(gl)
    for c in range(LANE_CHUNKS):
        hist_ref[pl.ds(prefix + c, tm, stride=LANE_CHUNKS), :] = v[:, c * LANES:(c + 1) * LANES]
    zg_ref[...] = _silu(_dot(h, win_ref[:, 2 * D_CONV:3 * D_CONV]))

    @pl.when(t == pl.num_programs(1) - 1)
    def _():
        st_ref[0] = v[tm - PREFIX_ROWS:tm, :]

    even_ref[...] = hist_ref[...].astype(BF16)
    odd_ref[...] = hist_ref[LANE_CHUNKS:total - LANE_CHUNKS, :].astype(BF16)

    lng = lng_ref[...]
    lnb = lnb_ref[...]
    rows = CONV_ROWS * LANE_CHUNKS
    for i in range(tm // CONV_ROWS):
        r0 = i * CONV_ROWS
        acc = None
        for k in range(CONV_WIDTH):
            tau = r0 + PREFIX_SKIP + k
            src, row = (even_ref, tau * LANE_CHUNKS) if tau % 2 == 0 else (odd_ref, (tau - 1) * LANE_CHUNKS)
            tap = src[row:row + rows, :].astype(F32).reshape(CONV_ROWS // 2, PAIR_ROWS, LANES)
            term = tap * w16_ref[k * PAIR_ROWS:(k + 1) * PAIR_ROWS, :].astype(F32)
            acc = term if acc is None else acc + term
        acc = acc.reshape(CONV_ROWS, LANE_CHUNKS, LANES) + bdw_ref[...]
        tmp_ref[i * rows:(i + 1) * rows, :] = acc.reshape(rows, LANES)
        conv = jnp.concatenate(
            [tmp_ref[pl.ds(i * rows + c, CONV_ROWS, stride=LANE_CHUNKS), :] for c in range(LANE_CHUNKS)], axis=1)
        mu = jnp.mean(conv, axis=-1, keepdims=True)
        xc = conv - mu
        var = jnp.mean(xc * xc, axis=-1, keepdims=True)
        y = xc * lax.rsqrt(var + LN_EPS) * lng + lnb
        act_ref[r0:r0 + CONV_ROWS, :] = (_silu(y) * zg_ref[r0:r0 + CONV_ROWS, :]).astype(BF16)

    y_ref[0] = x + _dot(act_ref[...], wout_ref[...])

    hprev_ref[...] = hist_ref[tm * LANE_CHUNKS:total, :]


def _mixer_a_prompt(x, g, w_in, w_dw, b_dw, ln_g, ln_b, w_out):
    b, t, d = x.shape
    tm = TILE_M
    const = lambda shape: pl.BlockSpec(shape, lambda i, j: (0,) * len(shape))
    w_pairs = jnp.tile(w_dw.reshape(CONV_WIDTH, 1, LANE_CHUNKS, LANES), (1, 2, 1, 1))
    w_pairs = w_pairs.reshape(CONV_WIDTH * PAIR_ROWS, LANES).astype(BF16)
    return pl.pallas_call(
        _mixer_a_prompt_kernel,
        grid=(b, t // tm),
        in_specs=[
            pl.BlockSpec((1, tm, d), lambda i, j: (i, j, 0)),
            const((1, d)),
            const((d, 3 * D_CONV)),
            const((CONV_WIDTH * PAIR_ROWS, LANES)),
            const((LANE_CHUNKS, LANES)),
            const((1, D_CONV)),
            const((1, D_CONV)),
            const((D_CONV, d)),
        ],
        out_specs=[
            pl.BlockSpec((1, tm, d), lambda i, j: (i, j, 0)),
            pl.BlockSpec((1, PREFIX_ROWS, D_CONV), lambda i, j: (i, 0, 0)),
        ],
        out_shape=[
            jax.ShapeDtypeStruct((b, t, d), F32),
            jax.ShapeDtypeStruct((b, PREFIX_ROWS, D_CONV), F32),
        ],
        scratch_shapes=[
            pltpu.VMEM(((PREFIX_ROWS + tm) * LANE_CHUNKS, LANES), F32),
            pltpu.VMEM((PREFIX_ROWS * LANE_CHUNKS, LANES), F32),
            pltpu.VMEM(((PREFIX_ROWS + tm) * LANE_CHUNKS, LANES), BF16),
            pltpu.VMEM(((PREFIX_ROWS + tm - 2) * LANE_CHUNKS, LANES), BF16),
            pltpu.VMEM((tm * LANE_CHUNKS, LANES), F32),
            pltpu.VMEM((tm, D_CONV), F32),
            pltpu.VMEM((tm, D_CONV), BF16),
        ],
        compiler_params=pltpu.CompilerParams(
            dimension_semantics=("arbitrary", "arbitrary"), vmem_limit_bytes=VMEM_LIMIT),
        name="mixer_a_prompt",
    )(x, g, w_in, w_pairs, b_dw.reshape(LANE_CHUNKS, LANES), ln_g, ln_b, w_out)


def _mixer_a_sample_kernel(x_ref, st_ref, g_ref, win_ref, wdw_ref, bdw_ref, lng_ref, lnb_ref, wout_ref,
                           y_ref, nst_ref, vf_ref, slab_ref, back_ref, zg_ref):
    nb, ts, d = x_ref.shape
    x = x_ref[...].reshape(nb * ts, d)
    h = (_rms_unit(x) * g_ref[...]).astype(BF16)
    a = _dot(h, win_ref[:, 0:D_CONV])
    gl = _dot(h, win_ref[:, D_CONV:2 * D_CONV])
    v = a * jax.nn.sigmoid(gl)
    zg_ref[...] = _silu(_dot(h, win_ref[:, 2 * D_CONV:3 * D_CONV]))

    for c in range(LANE_CHUNKS):
        slab_ref[c] = v[:, c * LANES:(c + 1) * LANES]
    vf_ref[0:CONV_STATE] = st_ref[...]
    for t in range(ts):
        for c in range(LANE_CHUNKS):
            vf_ref[CONV_STATE + t, :, c * LANES:(c + 1) * LANES] = slab_ref[c, pl.ds(t, nb, stride=ts), :]
    nst_ref[...] = vf_ref[ts:ts + CONV_STATE]

    bdw = bdw_ref[...]
    lng = lng_ref[...]
    lnb = lnb_ref[...]
    for b0 in range(0, nb, SUBLANES):
        acc = None
        for k in range(CONV_WIDTH):
            term = vf_ref[k:k + ts, b0:b0 + SUBLANES, :] * wdw_ref[k:k + 1, :]
            acc = term + bdw if acc is None else acc + term
        conv = acc.reshape(ts * SUBLANES, D_CONV)
        mu = jnp.mean(conv, axis=-1, keepdims=True)
        xc = conv - mu
        var = jnp.mean(xc * xc, axis=-1, keepdims=True)
        sy = _silu(xc * lax.rsqrt(var + LN_EPS) * lng + lnb)
        for t in range(ts):
            for c in range(LANE_CHUNKS):
                back_ref[c, pl.ds(b0 * ts + t, SUBLANES, stride=ts), :] = (
                    sy[t * SUBLANES:(t + 1) * SUBLANES, c * LANES:(c + 1) * LANES])

    act = jnp.concatenate([back_ref[c] for c in range(LANE_CHUNKS)], axis=1) * zg_ref[...]
    y_ref[...] = (x + _dot(act.astype(BF16), wout_ref[...])).reshape(nb, ts, d)


def _mixer_a_sample(x, state_t, g, w_in, w_dw, b_dw, ln_g, ln_b, w_out):
    b, ts, d = x.shape
    nb = SAMPLE_BLOCK
    const = lambda shape: pl.BlockSpec(shape, lambda i: (0,) * len(shape))
    return pl.pallas_call(
        _mixer_a_sample_kernel,
        grid=(b // nb,),
        in_specs=[
            pl.BlockSpec((nb, ts, d), lambda i: (i, 0, 0)),
            pl.BlockSpec((CONV_STATE, nb, D_CONV), lambda i: (0, i, 0)),
            const((1, d)),
            const((d, 3 * D_CONV)),
            const((CONV_WIDTH, D_CONV)),
            const((1, D_CONV)),
            const((1, D_CONV)),
            const((1, D_CONV)),
            const((D_CONV, d)),
        ],
        out_specs=[
            pl.BlockSpec((nb, ts, d), lambda i: (i, 0, 0)),
            pl.BlockSpec((CONV_STATE, nb, D_CONV), lambda i: (0, i, 0)),
        ],
        out_shape=[
            jax.ShapeDtypeStruct((b, ts, d), F32),
            jax.ShapeDtypeStruct((CONV_STATE, b, D_CONV), F32),
        ],
        scratch_shapes=[
            pltpu.VMEM((CONV_STATE + ts, nb, D_CONV), F32),
            pltpu.VMEM((LANE_CHUNKS, nb * ts, LANES), F32),
            pltpu.VMEM((LANE_CHUNKS, nb * ts, LANES), F32),
            pltpu.VMEM((nb * ts, D_CONV), F32),
        ],
        compiler_params=pltpu.CompilerParams(
            dimension_semantics=("arbitrary",), vmem_limit_bytes=VMEM_LIMIT),
        name="mixer_a_sample",
    )(x, state_t, g, w_in, w_dw, b_dw, ln_g, ln_b, w_out)


def _project_kvqz(x, gkv_ref, gb_ref, wkv_ref, wq_ref, wz_ref, kg_ref, qg_ref, seg_ref):
    xn = _rms_unit(x)
    hk = (xn * gkv_ref[...]).astype(BF16)
    hq = (xn * gb_ref[...]).astype(BF16)
    seg = seg_ref[...]
    kv = _dot(hk, wkv_ref[...])
    k = _head_rms(kv[:, 0:D_KV], seg, kg_ref[...])
    v = kv[:, D_KV:2 * D_KV]
    q_all = _dot(hq, wq_ref[...])
    q = []
    for g in range(GROUP):
        cols = slice(g * D_KV, (g + 1) * D_KV)
        q.append(_head_rms(q_all[:, cols], seg, qg_ref[...]))
    zg = _silu(_dot(hq, wz_ref[...]))
    return k, v, q, zg


def _lane_segment_mask(kvh):
    lane = lax.broadcasted_iota(jnp.int32, (1, D_KV), 1)
    return (lane >= kvh * HEAD_DIM) & (lane < (kvh + 1) * HEAD_DIM)


def _softmax_unnorm(s, sink):
    m = jnp.maximum(jnp.max(s, axis=-1, keepdims=True), sink)
    p = jnp.exp2(s - m)
    denom = jnp.sum(p, axis=-1, keepdims=True) + jnp.exp2(sink - m)
    return p, denom


def _mixer_b_prompt_kernel(sink_ref, x_ref, gkv_ref, gb_ref, wkv_ref, wq_ref, wz_ref, kg_ref, qg_ref,
                           seg_ref, bias_ref, wout_ref,
                           y_ref, kt_ref, vt_ref,
                           km_ref, vm_ref, kprev_ref, vprev_ref, q_ref, o_ref, hq_ref, zg_ref):
    t = pl.program_id(1)
    tm = x_ref.shape[1]

    zeros = jnp.zeros((N_KV, WINDOW, D_KV), BF16)
    km_ref[:, 0:WINDOW, :] = jnp.where(t > 0, kprev_ref[...], zeros)
    vm_ref[:, 0:WINDOW, :] = jnp.where(t > 0, vprev_ref[...], zeros)

    x = x_ref[0]
    xn = _rms_unit(x)
    hk = (xn * gkv_ref[...]).astype(BF16)
    hq_ref[...] = (xn * gb_ref[...]).astype(BF16)
    seg = seg_ref[...]
    kv = _dot(hk, wkv_ref[...])
    k = _head_rms(kv[:, 0:D_KV], seg, kg_ref[...])
    v = kv[:, D_KV:2 * D_KV]
    for kvh in range(N_KV):
        mask = _lane_segment_mask(kvh)
        km_ref[kvh, WINDOW:WINDOW + tm, :] = jnp.where(mask, k, 0.0).astype(BF16)
        vm_ref[kvh, WINDOW:WINDOW + tm, :] = jnp.where(mask, v, 0.0).astype(BF16)

    @pl.when(t == pl.num_programs(1) - 1)
    def _():
        kt_ref[0] = k[tm - WINDOW:tm, :]
        vt_ref[0] = v[tm - WINDOW:tm, :]

    span = WINDOW + Q_BLOCK

    def project_block(j):
        r0 = j * PROJ_ROWS
        hq = hq_ref[r0:r0 + PROJ_ROWS, :]
        q_all = _dot(hq, wq_ref[...])
        for g in range(GROUP):
            cols = slice(g * D_KV, (g + 1) * D_KV)
            q_ref[g, r0:r0 + PROJ_ROWS, :] = _head_rms(q_all[:, cols], seg, qg_ref[...]).astype(BF16)
        zg_ref[r0:r0 + PROJ_ROWS, :] = _silu(_dot(hq, wz_ref[...]))

    def attend_block(j):
        r0 = j * Q_BLOCK
        bsel = jnp.where(t == 0, 0, 1) if j == 0 else 1
        qstack = jnp.concatenate([q_ref[g, r0:r0 + Q_BLOCK, :] for g in range(GROUP)], axis=0)
        out = None
        keys = jnp.concatenate([km_ref[kvh, r0:r0 + span, :] for kvh in range(N_KV)], axis=0)
        s_all = _dot_nt(qstack, keys)
        for kvh in range(N_KV):
            s = s_all[:, kvh * span:(kvh + 1) * span] + bias_ref[bsel, kvh]
            probs = []
            for g in range(GROUP):
                p, denom = _softmax_unnorm(s[g * Q_BLOCK:(g + 1) * Q_BLOCK, :], sink_ref[kvh * GROUP + g])
                probs.append((p * (1.0 / denom)).astype(BF16))
            o = _dot(jnp.concatenate(probs, axis=0), vm_ref[kvh, r0:r0 + span, :])
            out = o if out is None else out + o
        for g in range(GROUP):
            o_ref[r0:r0 + Q_BLOCK, g * D_KV:(g + 1) * D_KV] = out[g * Q_BLOCK:(g + 1) * Q_BLOCK, :]

    def output_block(j):
        rows = slice(j * PROJ_ROWS, (j + 1) * PROJ_ROWS)
        gated = (o_ref[rows, :] * zg_ref[rows, :]).astype(BF16)
        y_ref[0, rows, :] = x_ref[0, rows, :] + _dot(gated, wout_ref[...])

    per_group = PROJ_ROWS // Q_BLOCK
    n_groups = tm // PROJ_ROWS
    project_block(0)
    for j in range(n_groups):
        if j + 1 < n_groups:
            project_block(j + 1)
        for jj in range(per_group):
            attend_block(j * per_group + jj)
        if j > 0:
            output_block(j - 1)
    output_block(n_groups - 1)

    for kvh in range(N_KV):
        mask = _lane_segment_mask(kvh)
        kprev_ref[kvh] = jnp.where(mask, k[tm - WINDOW:tm, :], 0.0).astype(BF16)
        vprev_ref[kvh] = jnp.where(mask, v[tm - WINDOW:tm, :], 0.0).astype(BF16)


def _mixer_b_prompt(x, sinks, gkv, gb, wkv, wq, wz, kg, qg, seg, bias, wout):
    b, t, d = x.shape
    tm = TILE_M
    const = lambda shape: pl.BlockSpec(shape, lambda i, j: (0,) * len(shape), pipeline_mode=pl.Buffered(1))
    return pl.pallas_call(
        _mixer_b_prompt_kernel,
        grid=(b, t // tm),
        in_specs=[
            pl.BlockSpec(memory_space=pltpu.SMEM),
            pl.BlockSpec((1, tm, d), lambda i, j: (i, j, 0)),
            const((1, d)),
            const((1, d)),
            const((d, 2 * D_KV)),
            const((d, D_ATTN)),
            const((d, D_ATTN)),
            const((1, D_KV)),
            const((1, D_KV)),
            const((D_KV, D_KV)),
            const((2, N_KV, GROUP * Q_BLOCK, WINDOW + Q_BLOCK)),
            const((D_ATTN, d)),
        ],
        out_specs=[
            pl.BlockSpec((1, tm, d), lambda i, j: (i, j, 0)),
            pl.BlockSpec((1, WINDOW, D_KV), lambda i, j: (i, 0, 0)),
            pl.BlockSpec((1, WINDOW, D_KV), lambda i, j: (i, 0, 0)),
        ],
        out_shape=[
            jax.ShapeDtypeStruct((b, t, d), F32),
            jax.ShapeDtypeStruct((b, WINDOW, D_KV), F32),
            jax.ShapeDtypeStruct((b, WINDOW, D_KV), F32),
        ],
        scratch_shapes=[
            pltpu.VMEM((N_KV, WINDOW + tm, D_KV), BF16),
            pltpu.VMEM((N_KV, WINDOW + tm, D_KV), BF16),
            pltpu.VMEM((N_KV, WINDOW, D_KV), BF16),
            pltpu.VMEM((N_KV, WINDOW, D_KV), BF16),
            pltpu.VMEM((GROUP, tm, D_KV), BF16),
            pltpu.VMEM((tm, D_ATTN), F32),
            pltpu.VMEM((tm, d), BF16),
            pltpu.VMEM((tm, D_ATTN), F32),
        ],
        compiler_params=pltpu.CompilerParams(
            dimension_semantics=("arbitrary", "arbitrary"), vmem_limit_bytes=VMEM_LIMIT),
        name="mixer_b_prompt",
    )(sinks, x, gkv, gb, wkv, wq, wz, kg, qg, seg, bias, wout)


def _mixer_b_sample_kernel(x_ref, ck_ref, cv_ref, gkv_ref, gb_ref, wkv_ref, wq_ref, wz_ref, kg_ref,
                           qg_ref, seg_ref, biasc_ref, biasn_ref, sink_ref, wout_ref,
                           y_ref, nk_ref, nv_ref,
                           kn_ref, vn_ref, q_ref, o_ref):
    nb, ts, d = x_ref.shape
    x = x_ref[...].reshape(nb * ts, d)
    k, v, q, zg = _project_kvqz(x, gkv_ref, gb_ref, wkv_ref, wq_ref, wz_ref, kg_ref, qg_ref, seg_ref)
    for g in range(GROUP):
        q_ref[g] = q[g].reshape(nb, ts, D_KV)
    kn_ref[...] = k.reshape(nb, ts, D_KV)
    vn_ref[...] = v.reshape(nb, ts, D_KV)

    masks = [_lane_segment_mask(kvh) for kvh in range(N_KV)]
    bias_c = biasc_ref[...]
    bias_n = biasn_ref[...]
    sink = sink_ref[...]
    new_lanes = lax.broadcasted_iota(jnp.int32, (1, WINDOW), 1) >= WINDOW - ts
    pad_rows = jnp.zeros((NEW_KEY_ROWS - ts, D_KV), F32)
    head_rows = jnp.zeros((WINDOW - ts, D_KV), F32)

    def seq(i, carry):
        k_old = ck_ref[i]
        v_old = cv_ref[i]
        k_new = kn_ref[i]
        v_new = vn_ref[i]
        rows = []
        for g in range(GROUP):
            qg_rows = q_ref[g, i]
            for kvh in range(N_KV):
                rows.append(jnp.where(masks[kvh], qg_rows, jnp.zeros_like(qg_rows)))
        lhs = jnp.concatenate(rows, axis=0).astype(BF16)
        k_new16 = jnp.concatenate([k_new, pad_rows], axis=0).astype(BF16)
        v_new16 = jnp.concatenate([v_new, pad_rows], axis=0).astype(BF16)
        s_c = _dot(lhs, k_old.astype(BF16)) + bias_c
        s_n = _dot_nt(lhs, k_new16) + bias_n
        m = jnp.maximum(jnp.maximum(jnp.max(s_c, axis=-1, keepdims=True), jnp.max(s_n, axis=-1, keepdims=True)),
                        sink)
        p_c = jnp.exp2(s_c - m)
        p_n = jnp.exp2(s_n - m)
        denom = (jnp.sum(p_c, axis=-1, keepdims=True) + jnp.sum(p_n, axis=-1, keepdims=True)
                 + jnp.exp2(sink - m))
        r = 1.0 / denom
        o = _dot_nt((p_c * r).astype(BF16), v_old.astype(BF16)) + _dot((p_n * r).astype(BF16), v_new16)
        for g in range(GROUP):
            acc = None
            for kvh in range(N_KV):
                r0 = (g * N_KV + kvh) * ts
                part = jnp.where(masks[kvh], o[r0:r0 + ts, :], 0.0)
                acc = part if acc is None else acc + part
            o_ref[i, :, g * D_KV:(g + 1) * D_KV] = acc
        k_cols = jnp.concatenate([head_rows, k_new], axis=0).T
        v_cols = jnp.concatenate([head_rows, v_new], axis=0).T
        nk_ref[i] = jnp.where(new_lanes, k_cols, pltpu.roll(k_old, WINDOW - ts, axis=1))
        nv_ref[i] = jnp.where(new_lanes, v_cols, pltpu.roll(v_old, WINDOW - ts, axis=1))
        return carry

    lax.fori_loop(0, nb, seq, 0, unroll=SEQ_UNROLL)

    og = (o_ref[...].reshape(nb * ts, D_ATTN) * zg).astype(BF16)
    y_ref[...] = (x + _dot(og, wout_ref[...])).reshape(nb, ts, d)


def _mixer_b_sample(x, ck_t, cv_t, gkv, gb, wkv, wq, wz, kg, qg, seg, bias_c, bias_n, sink, wout):
    b, ts, d = x.shape
    nb = SAMPLE_BLOCK_ATTN
    const = lambda shape: pl.BlockSpec(shape, lambda i: (0,) * len(shape))
    rows = N_HEADS * ts
    return pl.pallas_call(
        _mixer_b_sample_kernel,
        grid=(b // nb,),
        in_specs=[
            pl.BlockSpec((nb, ts, d), lambda i: (i, 0, 0)),
            pl.BlockSpec((nb, D_KV, WINDOW), lambda i: (i, 0, 0)),
            pl.BlockSpec((nb, D_KV, WINDOW), lambda i: (i, 0, 0)),
            const((1, d)),
            const((1, d)),
            const((d, 2 * D_KV)),
            const((d, D_ATTN)),
            const((d, D_ATTN)),
            const((1, D_KV)),
            const((1, D_KV)),
            const((D_KV, D_KV)),
            const((rows, WINDOW)),
            const((rows, NEW_KEY_ROWS)),
            const((rows, 1)),
            const((D_ATTN, d)),
        ],
        out_specs=[
            pl.BlockSpec((nb, ts, d), lambda i: (i, 0, 0)),
            pl.BlockSpec((nb, D_KV, WINDOW), lambda i: (i, 0, 0)),
            pl.BlockSpec((nb, D_KV, WINDOW), lambda i: (i, 0, 0)),
        ],
        out_shape=[
            jax.ShapeDtypeStruct((b, ts, d), F32),
            jax.ShapeDtypeStruct((b, D_KV, WINDOW), F32),
            jax.ShapeDtypeStruct((b, D_KV, WINDOW), F32),
        ],
        scratch_shapes=[
            pltpu.VMEM((nb, ts, D_KV), F32),
            pltpu.VMEM((nb, ts, D_KV), F32),
            pltpu.VMEM((GROUP, nb, ts, D_KV), F32),
            pltpu.VMEM((nb, ts, D_ATTN), F32),
        ],
        compiler_params=pltpu.CompilerParams(
            dimension_semantics=("arbitrary",), vmem_limit_bytes=VMEM_LIMIT),
        name="mixer_b_sample",
    )(x, ck_t, cv_t, gkv, gb, wkv, wq, wz, kg, qg, seg, bias_c, bias_n, sink, wout)


def _alibi_slopes():
    return 2.0 ** (-8.0 * np.arange(1, N_HEADS + 1) / N_HEADS)


def _band_bias(n_query, n_key, first_block):
    dist = np.arange(n_query)[:, None] + WINDOW - np.arange(n_key)[None, :]
    allowed = (dist >= 0) & (dist < WINDOW)
    if first_block:
        allowed = allowed & (np.arange(n_key)[None, :] >= WINDOW)
    bias = -_alibi_slopes()[:, None, None] * dist[None].astype(np.float64) * LOG2E
    return np.where(allowed[None], bias, -np.inf).astype(np.float32)


def _group_major(w, axis):
    shape = w.shape
    w = w.reshape(shape[:axis] + (N_KV, GROUP, HEAD_DIM) + shape[axis + 1:])
    w = jnp.swapaxes(w, axis, axis + 1)
    return w.reshape(shape)


def kernel(x_prompt, x_sample, state_conv, cache_k, cache_v, norm_a, w_in_a, w_dw_a, b_dw_a, ln_g_a, ln_b_a, w_out_a, norm_kv, w_kv, k_norm, norm_b, w_in_b, q_norm, sinks_b, w_out_b):
    assert norm_a.shape[0] == 1 and norm_b.shape[0] == 1
    row = lambda v: v.reshape(1, -1).astype(F32)

    a_args = (row(norm_a[0]), w_in_a[0].astype(BF16), w_dw_a[0].astype(F32), row(b_dw_a[0]),
              row(ln_g_a[0]), row(ln_b_a[0]), w_out_a[0].astype(BF16))
    xp, st_p = _mixer_a_prompt(x_prompt, *a_args)
    xs, st_s = _mixer_a_sample(x_sample, jnp.transpose(state_conv[0], (1, 0, 2)), *a_args)
    st_s = jnp.transpose(st_s, (1, 0, 2))[None]

    wq = _group_major(w_in_b[0][:, :D_ATTN], 1).astype(BF16)
    wz = _group_major(w_in_b[0][:, D_ATTN:], 1).astype(BF16)
    wout = _group_major(w_out_b[0], 0).astype(BF16)
    sinks = sinks_b[0].astype(F32) * LOG2E
    kg = row(jnp.tile(k_norm, N_KV))
    qg = row(jnp.tile(q_norm[0], N_KV)) * (HEAD_DIM ** -0.5 * LOG2E)
    seg = jnp.asarray(np.kron(np.eye(N_KV), np.full((HEAD_DIM, HEAD_DIM), 1.0 / HEAD_DIM)), BF16)
    shared = (row(norm_kv), row(norm_b[0]), w_kv.astype(BF16), wq, wz, kg, qg, seg)

    span = WINDOW + Q_BLOCK
    bias_p = jnp.asarray(np.stack([_band_bias(Q_BLOCK, span, True), _band_bias(Q_BLOCK, span, False)])
                         .reshape(2, N_KV, GROUP * Q_BLOCK, span))
    yp, k_tail, v_tail = _mixer_b_prompt(xp, sinks, *shared, bias_p, wout)

    ts = x_sample.shape[1]
    head_of_row = np.array([kvh * GROUP + g for g in range(GROUP) for kvh in range(N_KV)])
    bias_s = _band_bias(ts, WINDOW + NEW_KEY_ROWS, False)[head_of_row].reshape(N_HEADS * ts, WINDOW + NEW_KEY_ROWS)
    sink_s = jnp.repeat(sinks[head_of_row], ts).reshape(N_HEADS * ts, 1)
    nb = cache_k.shape[0]
    to_cols = lambda c: jnp.transpose(c, (0, 2, 3, 1)).reshape(nb, D_KV, WINDOW)
    from_cols = lambda c: jnp.transpose(c.reshape(nb, N_KV, HEAD_DIM, WINDOW), (0, 3, 1, 2))
    ys, nk, nv = _mixer_b_sample(xs, to_cols(cache_k), to_cols(cache_v), *shared,
                                 jnp.asarray(bias_s[:, :WINDOW]), jnp.asarray(bias_s[:, WINDOW:]), sink_s, wout)

    kv4 = lambda a: a.reshape(a.shape[0], WINDOW, N_KV, HEAD_DIM)
    return (yp, ys, st_p[None, :, PREFIX_SKIP:, :], st_s, kv4(k_tail), kv4(v_tail), from_cols(nk), from_cols(nv))
```

```python
import numpy as np
import jax
import jax.numpy as jnp
from jax import lax
from jax.experimental import pallas as pl
from jax.experimental.pallas import tpu as pltpu

F32 = jnp.float32
BF16 = jnp.bfloat16

D_MODEL = 1024
D_CONV = 1024
CONV_WIDTH = 31
CONV_STATE = CONV_WIDTH - 1
N_HEADS = 16
N_KV = 4
GROUP = N_HEADS // N_KV
HEAD_DIM = 64
D_ATTN = N_HEADS * HEAD_DIM
D_KV = N_KV * HEAD_DIM
WINDOW = 128
Q_BLOCK = 128
RMS_EPS = 1e-6
LN_EPS = 1e-5
LOG2E = float(np.log2(np.e))

SUBLANES = 8
LANES = 128
LANE_CHUNKS = D_CONV // LANES
assert LANE_CHUNKS == SUBLANES
BF16_ROWS = 16
PAIR_ROWS = 2 * LANE_CHUNKS
assert PAIR_ROWS == BF16_ROWS
PREFIX_ROWS = 32
PREFIX_SKIP = PREFIX_ROWS - CONV_STATE
TILE_M = 1024
CONV_ROWS = 32
SAMPLE_BLOCK = 32
SAMPLE_BLOCK_ATTN = 16
NEW_KEY_ROWS = 16
PROJ_ROWS = 256
OUT_ROWS = 512
SEQ_UNROLL = 8
VMEM_LIMIT = 56 * 1024 * 1024


def _dot(a, b):
    return jnp.dot(a, b, preferred_element_type=F32)


def _dot_nt(a, b):
    return lax.dot_general(a, b, (((1,), (1,)), ((), ())), preferred_element_type=F32)


def _rms_unit(x):
    return x * lax.rsqrt(jnp.mean(x * x, axis=-1, keepdims=True) + RMS_EPS)


def _silu(x):
    return x * jax.nn.sigmoid(x)


def _head_rms(x, seg_mean, gain):
    ms = _dot((x * x).astype(BF16), seg_mean)
    return x * lax.rsqrt(ms + RMS_EPS) * gain


def _mixer_a_prompt_kernel(x_ref, g_ref, win_ref, w16_ref, bdw_ref, lng_ref, lnb_ref, wout_ref,
                           y_ref, st_ref, hist_ref, hprev_ref, even_ref, odd_ref, tmp_ref, zg_ref, act_ref):
    t = pl.program_id(1)
    tm = x_ref.shape[1]
    prefix = PREFIX_ROWS * LANE_CHUNKS
    total = prefix + tm * LANE_CHUNKS

    hist_ref[0:prefix, :] = jnp.where(t > 0, hprev_ref[...], jnp.zeros((prefix, LANES), F32))

    x = x_ref[0]
    h = (_rms_unit(x) * g_ref[...]).astype(BF16)
    a = _dot(h, win_ref[:, 0:D_CONV])
    gl = _dot(h, win_ref[:, D_CONV:2 * D_CONV])
    v = a * jax.nn.sigmoid(gl)
    for c in range(LANE_CHUNKS):
        hist_ref[pl.ds(prefix + c, tm, stride=LANE_CHUNKS), :] = v[:, c * LANES:(c + 1) * LANES]
    zg_ref[...] = _silu(_dot(h, win_ref[:, 2 * D_CONV:3 * D_CONV]))

    @pl.when(t == pl.num_programs(1) - 1)
    def _():
        st_ref[0] = v[tm - PREFIX_ROWS:tm, :]

    even_ref[...] = hist_ref[...].astype(BF16)
    odd_ref[...] = hist_ref[LANE_CHUNKS:total - LANE_CHUNKS, :].astype(BF16)

    lng = lng_ref[...]
    lnb = lnb_ref[...]
    rows = CONV_ROWS * LANE_CHUNKS
    for i in range(tm // CONV_ROWS):
        r0 = i * CONV_ROWS
        acc = None
        for k in range(CONV_WIDTH):
            tau = r0 + PREFIX_SKIP + k
            src, row = (even_ref, tau * LANE_CHUNKS) if tau % 2 == 0 else (odd_ref, (tau - 1) * LANE_CHUNKS)
            tap = src[row:row + rows, :].astype(F32).reshape(CONV_ROWS // 2, PAIR_ROWS, LANES)
            term = tap * w16_ref[k * PAIR_ROWS:(k + 1) * PAIR_ROWS, :].astype(F32)
            acc = term if acc is None else acc + term
        acc = acc.reshape(CONV_ROWS, LANE_CHUNKS, LANES) + bdw_ref[...]
        tmp_ref[i * rows:(i + 1) * rows, :] = acc.reshape(rows, LANES)
        conv = jnp.concatenate(
            [tmp_ref[pl.ds(i * rows + c, CONV_ROWS, stride=LANE_CHUNKS), :] for c in range(LANE_CHUNKS)], axis=1)
        mu = jnp.mean(conv, axis=-1, keepdims=True)
        xc = conv - mu
        var = jnp.mean(xc * xc, axis=-1, keepdims=True)
        y = xc * lax.rsqrt(var + LN_EPS) * lng + lnb
        act_ref[r0:r0 + CONV_ROWS, :] = (_silu(y) * zg_ref[r0:r0 + CONV_ROWS, :]).astype(BF16)

    y_ref[0] = x + _dot(act_ref[...], wout_ref[...])

    hprev_ref[...] = hist_ref[tm * LANE_CHUNKS:total, :]


def _mixer_a_prompt(x, g, w_in, w_dw, b_dw, ln_g, ln_b, w_out):
    b, t, d = x.shape
    tm = TILE_M
    const = lambda shape: pl.BlockSpec(shape, lambda i, j: (0,) * len(shape))
    w_pairs = jnp.tile(w_dw.reshape(CONV_WIDTH, 1, LANE_CHUNKS, LANES), (1, 2, 1, 1))
    w_pairs = w_pairs.reshape(CONV_WIDTH * PAIR_ROWS, LANES).astype(BF16)
    return pl.pallas_call(
        _mixer_a_prompt_kernel,
        grid=(b, t // tm),
        in_specs=[
            pl.BlockSpec((1, tm, d), lambda i, j: (i, j, 0)),
            const((1, d)),
            const((d, 3 * D_CONV)),
            const((CONV_WIDTH * PAIR_ROWS, LANES)),
            const((LANE_CHUNKS, LANES)),
            const((1, D_CONV)),
            const((1, D_CONV)),
            const((D_CONV, d)),
        ],
        out_specs=[
            pl.BlockSpec((1, tm, d), lambda i, j: (i, j, 0)),
            pl.BlockSpec((1, PREFIX_ROWS, D_CONV), lambda i, j: (i, 0, 0)),
        ],
        out_shape=[
            jax.ShapeDtypeStruct((b, t, d), F32),
            jax.ShapeDtypeStruct((b, PREFIX_ROWS, D_CONV), F32),
        ],
        scratch_shapes=[
            pltpu.VMEM(((PREFIX_ROWS + tm) * LANE_CHUNKS, LANES), F32),
            pltpu.VMEM((PREFIX_ROWS * LANE_CHUNKS, LANES), F32),
            pltpu.VMEM(((PREFIX_ROWS + tm) * LANE_CHUNKS, LANES), BF16),
            pltpu.VMEM(((PREFIX_ROWS + tm - 2) * LANE_CHUNKS, LANES), BF16),
            pltpu.VMEM((tm * LANE_CHUNKS, LANES), F32),
            pltpu.VMEM((tm, D_CONV), F32),
            pltpu.VMEM((tm, D_CONV), BF16),
        ],
        compiler_params=pltpu.CompilerParams(
            dimension_semantics=("arbitrary", "arbitrary"), vmem_limit_bytes=VMEM_LIMIT),
        name="mixer_a_prompt",
    )(x, g, w_in, w_pairs, b_dw.reshape(LANE_CHUNKS, LANES), ln_g, ln_b, w_out)


def _mixer_a_sample_kernel(x_ref, st_ref, g_ref, win_ref, wdw_ref, bdw_ref, lng_ref, lnb_ref, wout_ref,
                           y_ref, nst_ref, vf_ref, slab_ref, back_ref, zg_ref):
    nb, ts, d = x_ref.shape
    x = x_ref[...].reshape(nb * ts, d)
    h = (_rms_unit(x) * g_ref[...]).astype(BF16)
    a = _dot(h, win_ref[:, 0:D_CONV])
    gl = _dot(h, win_ref[:, D_CONV:2 * D_CONV])
    v = a * jax.nn.sigmoid(gl)
    zg_ref[...] = _silu(_dot(h, win_ref[:, 2 * D_CONV:3 * D_CONV]))

    for c in range(LANE_CHUNKS):
        slab_ref[c] = v[:, c * LANES:(c + 1) * LANES]
    vf_ref[0:CONV_STATE] = st_ref[...]
    for t in range(ts):
        for c in range(LANE_CHUNKS):
            vf_ref[CONV_STATE + t, :, c * LANES:(c + 1) * LANES] = slab_ref[c, pl.ds(t, nb, stride=ts), :]
    nst_ref[...] = vf_ref[ts:ts + CONV_STATE]

    bdw = bdw_ref[...]
    lng = lng_ref[...]
    lnb = lnb_ref[...]
    for b0 in range(0, nb, SUBLANES):
        acc = None
        for k in range(CONV_WIDTH):
            term = vf_ref[k:k + ts, b0:b0 + SUBLANES, :] * wdw_ref[k:k + 1, :]
            acc = term + bdw if acc is None else acc + term
        conv = acc.reshape(ts * SUBLANES, D_CONV)
        mu = jnp.mean(conv, axis=-1, keepdims=True)
        xc = conv - mu
        var = jnp.mean(xc * xc, axis=-1, keepdims=True)
        sy = _silu(xc * lax.rsqrt(var + LN_EPS) * lng + lnb)
        for t in range(ts):
            for c in range(LANE_CHUNKS):
                back_ref[c, pl.ds(b0 * ts + t, SUBLANES, stride=ts), :] = (
                    sy[t * SUBLANES:(t + 1) * SUBLANES, c * LANES:(c + 1) * LANES])

    act = jnp.concatenate([back_ref[c] for c in range(LANE_CHUNKS)], axis=1) * zg_ref[...]
    y_ref[...] = (x + _dot(act.astype(BF16), wout_ref[...])).reshape(nb, ts, d)


def _mixer_a_sample(x, state_t, g, w_in, w_dw, b_dw, ln_g, ln_b, w_out):
    b, ts, d = x.shape
    nb = SAMPLE_BLOCK
    const = lambda shape: pl.BlockSpec(shape, lambda i: (0,) * len(shape))
    return pl.pallas_call(
        _mixer_a_sample_kernel,
        grid=(b // nb,),
        in_specs=[
            pl.BlockSpec((nb, ts, d), lambda i: (i, 0, 0)),
            pl.BlockSpec((CONV_STATE, nb, D_CONV), lambda i: (0, i, 0)),
            const((1, d)),
            const((d, 3 * D_CONV)),
            const((CONV_WIDTH, D_CONV)),
            const((1, D_CONV)),
            const((1, D_CONV)),
            const((1, D_CONV)),
            const((D_CONV, d)),
        ],
        out_specs=[
            pl.BlockSpec((nb, ts, d), lambda i: (i, 0, 0)),
            pl.BlockSpec((CONV_STATE, nb, D_CONV), lambda i: (0, i, 0)),
        ],
        out_shape=[
            jax.ShapeDtypeStruct((b, ts, d), F32),
            jax.ShapeDtypeStruct((CONV_STATE, b, D_CONV), F32),
        ],
        scratch_shapes=[
            pltpu.VMEM((CONV_STATE + ts, nb, D_CONV), F32),
            pltpu.VMEM((LANE_CHUNKS, nb * ts, LANES), F32),
            pltpu.VMEM((LANE_CHUNKS, nb * ts, LANES), F32),
            pltpu.VMEM((nb * ts, D_CONV), F32),
        ],
        compiler_params=pltpu.CompilerParams(
            dimension_semantics=("arbitrary",), vmem_limit_bytes=VMEM_LIMIT),
        name="mixer_a_sample",
    )(x, state_t, g, w_in, w_dw, b_dw, ln_g, ln_b, w_out)


def _project_kvqz(x, gkv_ref, gb_ref, wkv_ref, wq_ref, wz_ref, kg_ref, qg_ref, seg_ref):
    xn = _rms_unit(x)
    hk = (xn * gkv_ref[...]).astype(BF16)
    hq = (xn * gb_ref[...]).astype(BF16)
    seg = seg_ref[...]
    kv = _dot(hk, wkv_ref[...])
    k = _head_rms(kv[:, 0:D_KV], seg, kg_ref[...])
    v = kv[:, D_KV:2 * D_KV]
    q_all = _dot(hq, wq_ref[...])
    q = []
    for g in range(GROUP):
        cols = slice(g * D_KV, (g + 1) * D_KV)
        q.append(_head_rms(q_all[:, cols], seg, qg_ref[...]))
    zg = _silu(_dot(hq, wz_ref[...]))
    return k, v, q, zg


def _lane_segment_mask(kvh):
    lane = lax.broadcasted_iota(jnp.int32, (1, D_KV), 1)
    return (lane >= kvh * HEAD_DIM) & (lane < (kvh + 1) * HEAD_DIM)


def _softmax_unnorm(s, sink):
    m = jnp.maximum(jnp.max(s, axis=-1, keepdims=True), sink)
    p = jnp.exp2(s - m)
    denom = jnp.sum(p, axis=-1, keepdims=True) + jnp.exp2(sink - m)
    return p, denom


def _mixer_b_prompt_kernel(sink_ref, x_ref, gkv_ref, gb_ref, wkv_ref, wq_ref, wz_ref, kg_ref, qg_ref,
                           seg_ref, bias_ref, wout_ref,
                           y_ref, kt_ref, vt_ref,
                           km_ref, vm_ref, kprev_ref, vprev_ref, q_ref, o_ref, hq_ref, zg_ref):
    t = pl.program_id(1)
    tm = x_ref.shape[1]

    zeros = jnp.zeros((N_KV, WINDOW, D_KV), BF16)
    km_ref[:, 0:WINDOW, :] = jnp.where(t > 0, kprev_ref[...], zeros)
    vm_ref[:, 0:WINDOW, :] = jnp.where(t > 0, vprev_ref[...], zeros)

    x = x_ref[0]
    xn = _rms_unit(x)
    hk = (xn * gkv_ref[...]).astype(BF16)
    hq_ref[...] = (xn * gb_ref[...]).astype(BF16)
    seg = seg_ref[...]
    kv = _dot(hk, wkv_ref[...])
    k = _head_rms(kv[:, 0:D_KV], seg, kg_ref[...])
    v = kv[:, D_KV:2 * D_KV]
    for kvh in range(N_KV):
        mask = _lane_segment_mask(kvh)
        km_ref[kvh, WINDOW:WINDOW + tm, :] = jnp.where(mask, k, 0.0).astype(BF16)
        vm_ref[kvh, WINDOW:WINDOW + tm, :] = jnp.where(mask, v, 0.0).astype(BF16)

    @pl.when(t == pl.num_programs(1) - 1)
    def _():
        kt_ref[0] = k[tm - WINDOW:tm, :]
        vt_ref[0] = v[tm - WINDOW:tm, :]

    span = WINDOW + Q_BLOCK

    def project_block(j):
        r0 = j * PROJ_ROWS
        hq = hq_ref[r0:r0 + PROJ_ROWS, :]
        q_all = _dot(hq, wq_ref[...])
        for g in range(GROUP):
            cols = slice(g * D_KV, (g + 1) * D_KV)
            q_ref[g, r0:r0 + PROJ_ROWS, :] = _head_rms(q_all[:, cols], seg, qg_ref[...]).astype(BF16)
        zg_ref[r0:r0 + PROJ_ROWS, :] = _silu(_dot(hq, wz_ref[...]))

    def attend_block(j):
        r0 = j * Q_BLOCK
        bsel = jnp.where(t == 0, 0, 1) if j == 0 else 1
        qstack = jnp.concatenate([q_ref[g, r0:r0 + Q_BLOCK, :] for g in range(GROUP)], axis=0)
        out = None
        keys = jnp.concatenate([km_ref[kvh, r0:r0 + span, :] for kvh in range(N_KV)], axis=0)
        s_all = _dot_nt(qstack, keys)
        for kvh in range(N_KV):
            s = s_all[:, kvh * span:(kvh + 1) * span] + bias_ref[bsel, kvh]
            probs = []
            for g in range(GROUP):
                p, denom = _softmax_unnorm(s[g * Q_BLOCK:(g + 1) * Q_BLOCK, :], sink_ref[kvh * GROUP + g])
                probs.append((p * (1.0 / denom)).astype(BF16))
            o = _dot(jnp.concatenate(probs, axis=0), vm_ref[kvh, r0:r0 + span, :])
            out = o if out is None else out + o
        for g in range(GROUP):
            o_ref[r0:r0 + Q_BLOCK, g * D_KV:(g + 1) * D_KV] = out[g * Q_BLOCK:(g + 1) * Q_BLOCK, :]

    def output_block(j):
        rows = slice(j * OUT_ROWS, (j + 1) * OUT_ROWS)
        gated = (o_ref[rows, :] * zg_ref[rows, :]).astype(BF16)
        y_ref[0, rows, :] = x_ref[0, rows, :] + _dot(gated, wout_ref[...])

    per_proj = PROJ_ROWS // Q_BLOCK
    per_out = OUT_ROWS // Q_BLOCK
    project_block(0)
    for j in range(tm // Q_BLOCK):
        if j % per_proj == 0 and (j // per_proj + 1) * PROJ_ROWS < tm:
            project_block(j // per_proj + 1)
        attend_block(j)
        if j > 0 and j % per_out == 0:
            output_block(j // per_out - 1)
    output_block(tm // OUT_ROWS - 1)

    for kvh in range(N_KV):
        mask = _lane_segment_mask(kvh)
        kprev_ref[kvh] = jnp.where(mask, k[tm - WINDOW:tm, :], 0.0).astype(BF16)
        vprev_ref[kvh] = jnp.where(mask, v[tm - WINDOW:tm, :], 0.0).astype(BF16)


def _mixer_b_prompt(x, sinks, gkv, gb, wkv, wq, wz, kg, qg, seg, bias, wout):
    b, t, d = x.shape
    tm = TILE_M
    const = lambda shape: pl.BlockSpec(shape, lambda i, j: (0,) * len(shape), pipeline_mode=pl.Buffered(1))
    return pl.pallas_call(
        _mixer_b_prompt_kernel,
        grid=(b, t // tm),
        in_specs=[
            pl.BlockSpec(memory_space=pltpu.SMEM),
            pl.BlockSpec((1, tm, d), lambda i, j: (i, j, 0)),
            const((1, d)),
            const((1, d)),
            const((d, 2 * D_KV)),
            const((d, D_ATTN)),
            const((d, D_ATTN)),
            const((1, D_KV)),
            const((1, D_KV)),
            const((D_KV, D_KV)),
            const((2, N_KV, GROUP * Q_BLOCK, WINDOW + Q_BLOCK)),
            const((D_ATTN, d)),
        ],
        out_specs=[
            pl.BlockSpec((1, tm, d), lambda i, j: (i, j, 0)),
            pl.BlockSpec((1, WINDOW, D_KV), lambda i, j: (i, 0, 0)),
            pl.BlockSpec((1, WINDOW, D_KV), lambda i, j: (i, 0, 0)),
        ],
        out_shape=[
            jax.ShapeDtypeStruct((b, t, d), F32),
            jax.ShapeDtypeStruct((b, WINDOW, D_KV), F32),
            jax.ShapeDtypeStruct((b, WINDOW, D_KV), F32),
        ],
        scratch_shapes=[
            pltpu.VMEM((N_KV, WINDOW + tm, D_KV), BF16),
            pltpu.VMEM((N_KV, WINDOW + tm, D_KV), BF16),
            pltpu.VMEM((N_KV, WINDOW, D_KV), BF16),
            pltpu.VMEM((N_KV, WINDOW, D_KV), BF16),
            pltpu.VMEM((GROUP, tm, D_KV), BF16),
            pltpu.VMEM((tm, D_ATTN), F32),
            pltpu.VMEM((tm, d), BF16),
            pltpu.VMEM((tm, D_ATTN), F32),
        ],
        compiler_params=pltpu.CompilerParams(
            dimension_semantics=("arbitrary", "arbitrary"), vmem_limit_bytes=VMEM_LIMIT),
        name="mixer_b_prompt",
    )(sinks, x, gkv, gb, wkv, wq, wz, kg, qg, seg, bias, wout)


def _mixer_b_sample_kernel(x_ref, ck_ref, cv_ref, gkv_ref, gb_ref, wkv_ref, wq_ref, wz_ref, kg_ref,
                           qg_ref, seg_ref, biasc_ref, biasn_ref, sink_ref, wout_ref,
                           y_ref, nk_ref, nv_ref,
                           kn_ref, vn_ref, q_ref, o_ref):
    nb, ts, d = x_ref.shape
    x = x_ref[...].reshape(nb * ts, d)
    k, v, q, zg = _project_kvqz(x, gkv_ref, gb_ref, wkv_ref, wq_ref, wz_ref, kg_ref, qg_ref, seg_ref)
    for g in range(GROUP):
        q_ref[g] = q[g].reshape(nb, ts, D_KV)
    kn_ref[...] = k.reshape(nb, ts, D_KV)
    vn_ref[...] = v.reshape(nb, ts, D_KV)

    masks = [_lane_segment_mask(kvh) for kvh in range(N_KV)]
    bias_c = biasc_ref[...]
    bias_n = biasn_ref[...]
    sink = sink_ref[...]
    new_lanes = lax.broadcasted_iota(jnp.int32, (1, WINDOW), 1) >= WINDOW - ts
    pad_rows = jnp.zeros((NEW_KEY_ROWS - ts, D_KV), F32)
    head_rows = jnp.zeros((WINDOW - ts, D_KV), F32)

    def seq(i, carry):
        k_old = ck_ref[i]
        v_old = cv_ref[i]
        k_new = kn_ref[i]
        v_new = vn_ref[i]
        rows = []
        for g in range(GROUP):
            qg_rows = q_ref[g, i]
            for kvh in range(N_KV):
                rows.append(jnp.where(masks[kvh], qg_rows, jnp.zeros_like(qg_rows)))
        lhs = jnp.concatenate(rows, axis=0).astype(BF16)
        k_new16 = jnp.concatenate([k_new, pad_rows], axis=0).astype(BF16)
        v_new16 = jnp.concatenate([v_new, pad_rows], axis=0).astype(BF16)
        s_c = _dot(lhs, k_old.astype(BF16)) + bias_c
        s_n = _dot_nt(lhs, k_new16) + bias_n
        m = jnp.maximum(jnp.maximum(jnp.max(s_c, axis=-1, keepdims=True), jnp.max(s_n, axis=-1, keepdims=True)),
                        sink)
        p_c = jnp.exp2(s_c - m)
        p_n = jnp.exp2(s_n - m)
        denom = (jnp.sum(p_c, axis=-1, keepdims=True) + jnp.sum(p_n, axis=-1, keepdims=True)
                 + jnp.exp2(sink - m))
        r = 1.0 / denom
        o = _dot_nt((p_c * r).astype(BF16), v_old.astype(BF16)) + _dot((p_n * r).astype(BF16), v_new16)
        for g in range(GROUP):
            acc = None
            for kvh in range(N_KV):
                r0 = (g * N_KV + kvh) * ts
                part = jnp.where(masks[kvh], o[r0:r0 + ts, :], 0.0)
                acc = part if acc is None else acc + part
            o_ref[i, :, g * D_KV:(g + 1) * D_KV] = acc
        k_cols = jnp.concatenate([head_rows, k_new], axis=0).T
        v_cols = jnp.concatenate([head_rows, v_new], axis=0).T
        nk_ref[i] = jnp.where(new_lanes, k_cols, pltpu.roll(k_old, WINDOW - ts, axis=1))
        nv_ref[i] = jnp.where(new_lanes, v_cols, pltpu.roll(v_old, WINDOW - ts, axis=1))
        return carry

    lax.fori_loop(0, nb, seq, 0, unroll=SEQ_UNROLL)

    og = (o_ref[...].reshape(nb * ts, D_ATTN) * zg).astype(BF16)
    y_ref[...] = (x + _dot(og, wout_ref[...])).reshape(nb, ts, d)


def _mixer_b_sample(x, ck_t, cv_t, gkv, gb, wkv, wq, wz, kg, qg, seg, bias_c, bias_n, sink, wout):
    b, ts, d = x.shape
    nb = SAMPLE_BLOCK_ATTN
    const = lambda shape: pl.BlockSpec(shape, lambda i: (0,) * len(shape))
    rows = N_HEADS * ts
    return pl.pallas_call(
        _mixer_b_sample_kernel,
        grid=(b // nb,),
        in_specs=[
            pl.BlockSpec((nb, ts, d), lambda i: (i, 0, 0)),
            pl.BlockSpec((nb, D_KV, WINDOW), lambda i: (i, 0, 0)),
            pl.BlockSpec((nb, D_KV, WINDOW), lambda i: (i, 0, 0)),
            const((1, d)),
            const((1, d)),
            const((d, 2 * D_KV)),
            const((d, D_ATTN)),
            const((d, D_ATTN)),
            const((1, D_KV)),
            const((1, D_KV)),
            const((D_KV, D_KV)),
            const((rows, WINDOW)),
            const((rows, NEW_KEY_ROWS)),
            const((rows, 1)),
            const((D_ATTN, d)),
        ],
        out_specs=[
            pl.BlockSpec((nb, ts, d), lambda i: (i, 0, 0)),
            pl.BlockSpec((nb, D_KV, WINDOW), lambda i: (i, 0, 0)),
            pl.BlockSpec((nb, D_KV, WINDOW), lambda i: (i, 0, 0)),
        ],
        out_shape=[
            jax.ShapeDtypeStruct((b, ts, d), F32),
            jax.ShapeDtypeStruct((b, D_KV, WINDOW), F32),
            jax.ShapeDtypeStruct((b, D_KV, WINDOW), F32),
        ],
        scratch_shapes=[
            pltpu.VMEM((nb, ts, D_KV), F32),
            pltpu.VMEM((nb, ts, D_KV), F32),
            pltpu.VMEM((GROUP, nb, ts, D_KV), F32),
            pltpu.VMEM((nb, ts, D_ATTN), F32),
        ],
        compiler_params=pltpu.CompilerParams(
            dimension_semantics=("arbitrary",), vmem_limit_bytes=VMEM_LIMIT),
        name="mixer_b_sample",
    )(x, ck_t, cv_t, gkv, gb, wkv, wq, wz, kg, qg, seg, bias_c, bias_n, sink, wout)


def _alibi_slopes():
    return 2.0 ** (-8.0 * np.arange(1, N_HEADS + 1) / N_HEADS)


def _band_bias(n_query, n_key, first_block):
    dist = np.arange(n_query)[:, None] + WINDOW - np.arange(n_key)[None, :]
    allowed = (dist >= 0) & (dist < WINDOW)
    if first_block:
        allowed = allowed & (np.arange(n_key)[None, :] >= WINDOW)
    bias = -_alibi_slopes()[:, None, None] * dist[None].astype(np.float64) * LOG2E
    return np.where(allowed[None], bias, -np.inf).astype(np.float32)


def _group_major(w, axis):
    shape = w.shape
    w = w.reshape(shape[:axis] + (N_KV, GROUP, HEAD_DIM) + shape[axis + 1:])
    w = jnp.swapaxes(w, axis, axis + 1)
    return w.reshape(shape)


def kernel(x_prompt, x_sample, state_conv, cache_k, cache_v, norm_a, w_in_a, w_dw_a, b_dw_a, ln_g_a, ln_b_a, w_out_a, norm_kv, w_kv, k_norm, norm_b, w_in_b, q_norm, sinks_b, w_out_b):
    assert norm_a.shape[0] == 1 and norm_b.shape[0] == 1
    row = lambda v: v.reshape(1, -1).astype(F32)

    a_args = (row(norm_a[0]), w_in_a[0].astype(BF16), w_dw_a[0].astype(F32), row(b_dw_a[0]),
              row(ln_g_a[0]), row(ln_b_a[0]), w_out_a[0].astype(BF16))
    xp, st_p = _mixer_a_prompt(x_prompt, *a_args)
    xs, st_s = _mixer_a_sample(x_sample, jnp.transpose(state_conv[0], (1, 0, 2)), *a_args)
    st_s = jnp.transpose(st_s, (1, 0, 2))[None]

    wq = _group_major(w_in_b[0][:, :D_ATTN], 1).astype(BF16)
    wz = _group_major(w_in_b[0][:, D_ATTN:], 1).astype(BF16)
    wout = _group_major(w_out_b[0], 0).astype(BF16)
    sinks = sinks_b[0].astype(F32) * LOG2E
    kg = row(jnp.tile(k_norm, N_KV))
    qg = row(jnp.tile(q_norm[0], N_KV)) * (HEAD_DIM ** -0.5 * LOG2E)
    seg = jnp.asarray(np.kron(np.eye(N_KV), np.full((HEAD_DIM, HEAD_DIM), 1.0 / HEAD_DIM)), BF16)
    shared = (row(norm_kv), row(norm_b[0]), w_kv.astype(BF16), wq, wz, kg, qg, seg)

    span = WINDOW + Q_BLOCK
    bias_p = jnp.asarray(np.stack([_band_bias(Q_BLOCK, span, True), _band_bias(Q_BLOCK, span, False)])
                         .reshape(2, N_KV, GROUP * Q_BLOCK, span))
    yp, k_tail, v_tail = _mixer_b_prompt(xp, sinks, *shared, bias_p, wout)

    ts = x_sample.shape[1]
    head_of_row = np.array([kvh * GROUP + g for g in range(GROUP) for kvh in range(N_KV)])
    bias_s = _band_bias(ts, WINDOW + NEW_KEY_ROWS, False)[head_of_row].reshape(N_HEADS * ts, WINDOW + NEW_KEY_ROWS)
    sink_s = jnp.repeat(sinks[head_of_row], ts).reshape(N_HEADS * ts, 1)
    nb = cache_k.shape[0]
    to_cols = lambda c: jnp.transpose(c, (0, 2, 3, 1)).reshape(nb, D_KV, WINDOW)
    from_cols = lambda c: jnp.transpose(c.reshape(nb, N_KV, HEAD_DIM, WINDOW), (0, 3, 1, 2))
    ys, nk, nv = _mixer_b_sample(xs, to_cols(cache_k), to_cols(cache_v), *shared,
                                 jnp.asarray(bias_s[:, :WINDOW]), jnp.asarray(bias_s[:, WINDOW:]), sink_s, wout)

    kv4 = lambda a: a.reshape(a.shape[0], WINDOW, N_KV, HEAD_DIM)
    return (yp, ys, st_p[None, :, PREFIX_SKIP:, :], st_s, kv4(k_tail), kv4(v_tail), from_cols(nk), from_cols(nv))
```

```python
import numpy as np
import jax
import jax.numpy as jnp
from jax import lax
from jax.experimental import pallas as pl
from jax.experimental.pallas import tpu as pltpu

F32 = jnp.float32
BF16 = jnp.bfloat16

D_MODEL = 1024
D_CONV = 1024
CONV_WIDTH = 31
CONV_STATE = CONV_WIDTH - 1
N_HEADS = 16
N_KV = 4
GROUP = N_HEADS // N_KV
HEAD_DIM = 64
D_ATTN = N_HEADS * HEAD_DIM
D_KV = N_KV * HEAD_DIM
WINDOW = 128
Q_BLOCK = 128
RMS_EPS = 1e-6
LN_EPS = 1e-5
LOG2E = float(np.log2(np.e))

SUBLANES = 8
LANES = 128
LANE_CHUNKS = D_CONV // LANES
assert LANE_CHUNKS == SUBLANES
BF16_ROWS = 16
PAIR_ROWS = 2 * LANE_CHUNKS
assert PAIR_ROWS == BF16_ROWS
PREFIX_ROWS = 32
PREFIX_SKIP = PREFIX_ROWS - CONV_STATE
TILE_M = 1024
CONV_ROWS = 32
SAMPLE_BLOCK = 32
SAMPLE_BLOCK_ATTN = 16
NEW_KEY_ROWS = 16
PROJ_ROWS = 256
OUT_ROWS = 512
SEQ_UNROLL = 8
VMEM_LIMIT = 56 * 1024 * 1024


def _dot(a, b):
    return jnp.dot(a, b, preferred_element_type=F32)


def _dot_nt(a, b):
    return lax.dot_general(a, b, (((1,), (1,)), ((), ())), preferred_element_type=F32)


def _rms_unit(x):
    return x * lax.rsqrt(jnp.mean(x * x, axis=-1, keepdims=True) + RMS_EPS)


def _silu(x):
    return x * jax.nn.sigmoid(x)


def _head_rms(x, seg_mean, gain):
    ms = _dot((x * x).astype(BF16), seg_mean)
    return x * lax.rsqrt(ms + RMS_EPS) * gain


def _mixer_a_prompt_kernel(x_ref, g_ref, win_ref, w16_ref, bdw_ref, lng_ref, lnb_ref, wout_ref,
                           y_ref, st_ref, hist_ref, hprev_ref, even_ref, odd_ref, tmp_ref, zg_ref, act_ref):
    t = pl.program_id(1)
    tm = x_ref.shape[1]
    prefix = PREFIX_ROWS * LANE_CHUNKS
    total = prefix + tm * LANE_CHUNKS

    hist_ref[0:prefix, :] = jnp.where(t > 0, hprev_ref[...], jnp.zeros((prefix, LANES), F32))

    x = x_ref[0]
    h = (_rms_unit(x) * g_ref[...]).astype(BF16)
    a = _dot(h, win_ref[:, 0:D_CONV])
    gl = _dot(h, win_ref[:, D_CONV:2 * D_CONV])
    v = a * jax.nn.sigmoid(gl)
    for c in range(LANE_CHUNKS):
        hist_ref[pl.ds(prefix + c, tm, stride=LANE_CHUNKS), :] = v[:, c * LANES:(c + 1) * LANES]
    zg_ref[...] = _silu(_dot(h, win_ref[:, 2 * D_CONV:3 * D_CONV]))

    @pl.when(t == pl.num_programs(1) - 1)
    def _():
        st_ref[0] = v[tm - PREFIX_ROWS:tm, :]

    even_ref[...] = hist_ref[...].astype(BF16)
    odd_ref[...] = hist_ref[LANE_CHUNKS:total - LANE_CHUNKS, :].astype(BF16)

    lng = lng_ref[...]
    lnb = lnb_ref[...]
    rows = CONV_ROWS * LANE_CHUNKS
    for i in range(tm // CONV_ROWS):
        r0 = i * CONV_ROWS
        acc = None
        for k in range(CONV_WIDTH):
            tau = r0 + PREFIX_SKIP + k
            src, row = (even_ref, tau * LANE_CHUNKS) if tau % 2 == 0 else (odd_ref, (tau - 1) * LANE_CHUNKS)
            tap = src[row:row + rows, :].astype(F32).reshape(CONV_ROWS // 2, PAIR_ROWS, LANES)
            term = tap * w16_ref[k * PAIR_ROWS:(k + 1) * PAIR_ROWS, :].astype(F32)
            acc = term if acc is None else acc + term
        acc = acc.reshape(CONV_ROWS, LANE_CHUNKS, LANES) + bdw_ref[...]
        tmp_ref[i * rows:(i + 1) * rows, :] = acc.reshape(rows, LANES)
        conv = jnp.concatenate(
            [tmp_ref[pl.ds(i * rows + c, CONV_ROWS, stride=LANE_CHUNKS), :] for c in range(LANE_CHUNKS)], axis=1)
        mu = jnp.mean(conv, axis=-1, keepdims=True)
        xc = conv - mu
        var = jnp.mean(xc * xc, axis=-1, keepdims=True)
        y = xc * lax.rsqrt(var + LN_EPS) * lng + lnb
        act_ref[r0:r0 + CONV_ROWS, :] = (_silu(y) * zg_ref[r0:r0 + CONV_ROWS, :]).astype(BF16)

    y_ref[0] = x + _dot(act_ref[...], wout_ref[...])

    hprev_ref[...] = hist_ref[tm * LANE_CHUNKS:total, :]


def _mixer_a_prompt(x, g, w_in, w_dw, b_dw, ln_g, ln_b, w_out):
    b, t, d = x.shape
    tm = TILE_M
    const = lambda shape: pl.BlockSpec(shape, lambda i, j: (0,) * len(shape))
    w_pairs = jnp.tile(w_dw.reshape(CONV_WIDTH, 1, LANE_CHUNKS, LANES), (1, 2, 1, 1))
    w_pairs = w_pairs.reshape(CONV_WIDTH * PAIR_ROWS, LANES).astype(BF16)
    return pl.pallas_call(
        _mixer_a_prompt_kernel,
        grid=(b, t // tm),
        in_specs=[
            pl.BlockSpec((1, tm, d), lambda i, j: (i, j, 0)),
            const((1, d)),
            const((d, 3 * D_CONV)),
            const((CONV_WIDTH * PAIR_ROWS, LANES)),
            const((LANE_CHUNKS, LANES)),
            const((1, D_CONV)),
            const((1, D_CONV)),
            const((D_CONV, d)),
        ],
        out_specs=[
            pl.BlockSpec((1, tm, d), lambda i, j: (i, j, 0)),
            pl.BlockSpec((1, PREFIX_ROWS, D_CONV), lambda i, j: (i, 0, 0)),
        ],
        out_shape=[
            jax.ShapeDtypeStruct((b, t, d), F32),
            jax.ShapeDtypeStruct((b, PREFIX_ROWS, D_CONV), F32),
        ],
        scratch_shapes=[
            pltpu.VMEM(((PREFIX_ROWS + tm) * LANE_CHUNKS, LANES), F32),
            pltpu.VMEM((PREFIX_ROWS * LANE_CHUNKS, LANES), F32),
            pltpu.VMEM(((PREFIX_ROWS + tm) * LANE_CHUNKS, LANES), BF16),
            pltpu.VMEM(((PREFIX_ROWS + tm - 2) * LANE_CHUNKS, LANES), BF16),
            pltpu.VMEM((tm * LANE_CHUNKS, LANES), F32),
            pltpu.VMEM((tm, D_CONV), F32),
            pltpu.VMEM((tm, D_CONV), BF16),
        ],
        compiler_params=pltpu.CompilerParams(
            dimension_semantics=("arbitrary", "arbitrary"), vmem_limit_bytes=VMEM_LIMIT,
            allow_input_fusion=[False, False, True, True, False, False, False, True]),
        name="mixer_a_prompt",
    )(x, g, w_in, w_pairs, b_dw.reshape(LANE_CHUNKS, LANES), ln_g, ln_b, w_out)


def _mixer_a_sample_kernel(x_ref, st_ref, g_ref, win_ref, wdw_ref, bdw_ref, lng_ref, lnb_ref, wout_ref,
                           y_ref, nst_ref, vf_ref, slab_ref, back_ref, zg_ref):
    nb, ts, d = x_ref.shape
    x = x_ref[...].reshape(nb * ts, d)
    h = (_rms_unit(x) * g_ref[...]).astype(BF16)
    a = _dot(h, win_ref[:, 0:D_CONV])
    gl = _dot(h, win_ref[:, D_CONV:2 * D_CONV])
    v = a * jax.nn.sigmoid(gl)
    zg_ref[...] = _silu(_dot(h, win_ref[:, 2 * D_CONV:3 * D_CONV]))

    for c in range(LANE_CHUNKS):
        slab_ref[c] = v[:, c * LANES:(c + 1) * LANES]
    vf_ref[0:CONV_STATE] = st_ref[...]
    for t in range(ts):
        for c in range(LANE_CHUNKS):
            vf_ref[CONV_STATE + t, :, c * LANES:(c + 1) * LANES] = slab_ref[c, pl.ds(t, nb, stride=ts), :]
    nst_ref[...] = vf_ref[ts:ts + CONV_STATE]

    bdw = bdw_ref[...]
    lng = lng_ref[...]
    lnb = lnb_ref[...]
    for b0 in range(0, nb, SUBLANES):
        acc = None
        for k in range(CONV_WIDTH):
            term = vf_ref[k:k + ts, b0:b0 + SUBLANES, :] * wdw_ref[k:k + 1, :]
            acc = term + bdw if acc is None else acc + term
        conv = acc.reshape(ts * SUBLANES, D_CONV)
        mu = jnp.mean(conv, axis=-1, keepdims=True)
        xc = conv - mu
        var = jnp.mean(xc * xc, axis=-1, keepdims=True)
        sy = _silu(xc * lax.rsqrt(var + LN_EPS) * lng + lnb)
        for t in range(ts):
            for c in range(LANE_CHUNKS):
                back_ref[c, pl.ds(b0 * ts + t, SUBLANES, stride=ts), :] = (
                    sy[t * SUBLANES:(t + 1) * SUBLANES, c * LANES:(c + 1) * LANES])

    act = jnp.concatenate([back_ref[c] for c in range(LANE_CHUNKS)], axis=1) * zg_ref[...]
    y_ref[...] = (x + _dot(act.astype(BF16), wout_ref[...])).reshape(nb, ts, d)


def _mixer_a_sample(x, state_t, g, w_in, w_dw, b_dw, ln_g, ln_b, w_out):
    b, ts, d = x.shape
    nb = SAMPLE_BLOCK
    const = lambda shape: pl.BlockSpec(shape, lambda i: (0,) * len(shape))
    return pl.pallas_call(
        _mixer_a_sample_kernel,
        grid=(b // nb,),
        in_specs=[
            pl.BlockSpec((nb, ts, d), lambda i: (i, 0, 0)),
            pl.BlockSpec((CONV_STATE, nb, D_CONV), lambda i: (0, i, 0)),
            const((1, d)),
            const((d, 3 * D_CONV)),
            const((CONV_WIDTH, D_CONV)),
            const((1, D_CONV)),
            const((1, D_CONV)),
            const((1, D_CONV)),
            const((D_CONV, d)),
        ],
        out_specs=[
            pl.BlockSpec((nb, ts, d), lambda i: (i, 0, 0)),
            pl.BlockSpec((CONV_STATE, nb, D_CONV), lambda i: (0, i, 0)),
        ],
        out_shape=[
            jax.ShapeDtypeStruct((b, ts, d), F32),
            jax.ShapeDtypeStruct((CONV_STATE, b, D_CONV), F32),
        ],
        scratch_shapes=[
            pltpu.VMEM((CONV_STATE + ts, nb, D_CONV), F32),
            pltpu.VMEM((LANE_CHUNKS, nb * ts, LANES), F32),
            pltpu.VMEM((LANE_CHUNKS, nb * ts, LANES), F32),
            pltpu.VMEM((nb * ts, D_CONV), F32),
        ],
        compiler_params=pltpu.CompilerParams(
            dimension_semantics=("arbitrary",), vmem_limit_bytes=VMEM_LIMIT),
        name="mixer_a_sample",
    )(x, state_t, g, w_in, w_dw, b_dw, ln_g, ln_b, w_out)


def _project_kvqz(x, gkv_ref, gb_ref, wkv_ref, wq_ref, wz_ref, kg_ref, qg_ref, seg_ref):
    xn = _rms_unit(x)
    hk = (xn * gkv_ref[...]).astype(BF16)
    hq = (xn * gb_ref[...]).astype(BF16)
    seg = seg_ref[...]
    kv = _dot(hk, wkv_ref[...])
    k = _head_rms(kv[:, 0:D_KV], seg, kg_ref[...])
    v = kv[:, D_KV:2 * D_KV]
    q_all = _dot(hq, wq_ref[...])
    q = []
    for g in range(GROUP):
        cols = slice(g * D_KV, (g + 1) * D_KV)
        q.append(_head_rms(q_all[:, cols], seg, qg_ref[...]))
    zg = _silu(_dot(hq, wz_ref[...]))
    return k, v, q, zg


def _lane_segment_mask(kvh):
    lane = lax.broadcasted_iota(jnp.int32, (1, D_KV), 1)
    return (lane >= kvh * HEAD_DIM) & (lane < (kvh + 1) * HEAD_DIM)


def _softmax_unnorm(s, sink):
    m = jnp.maximum(jnp.max(s, axis=-1, keepdims=True), sink)
    p = jnp.exp2(s - m)
    denom = jnp.sum(p, axis=-1, keepdims=True) + jnp.exp2(sink - m)
    return p, denom


def _mixer_b_prompt_kernel(sink_ref, x_ref, gkv_ref, gb_ref, wkv_ref, wq_ref, wz_ref, kg_ref, qg_ref,
                           seg_ref, bias_ref, wout_ref,
                           y_ref, kt_ref, vt_ref,
                           km_ref, vm_ref, kprev_ref, vprev_ref, q_ref, o_ref, hq_ref, zg_ref):
    t = pl.program_id(1)
    tm = x_ref.shape[1]

    zeros = jnp.zeros((N_KV, WINDOW, D_KV), BF16)
    km_ref[:, 0:WINDOW, :] = jnp.where(t > 0, kprev_ref[...], zeros)
    vm_ref[:, 0:WINDOW, :] = jnp.where(t > 0, vprev_ref[...], zeros)

    x = x_ref[0]
    xn = _rms_unit(x)
    hk = (xn * gkv_ref[...]).astype(BF16)
    hq_ref[...] = (xn * gb_ref[...]).astype(BF16)
    seg = seg_ref[...]
    kv = _dot(hk, wkv_ref[...])
    k = _head_rms(kv[:, 0:D_KV], seg, kg_ref[...])
    v = kv[:, D_KV:2 * D_KV]
    for kvh in range(N_KV):
        mask = _lane_segment_mask(kvh)
        km_ref[kvh, WINDOW:WINDOW + tm, :] = jnp.where(mask, k, 0.0).astype(BF16)
        vm_ref[kvh, WINDOW:WINDOW + tm, :] = jnp.where(mask, v, 0.0).astype(BF16)

    @pl.when(t == pl.num_programs(1) - 1)
    def _():
        kt_ref[0] = k[tm - WINDOW:tm, :]
        vt_ref[0] = v[tm - WINDOW:tm, :]

    span = WINDOW + Q_BLOCK

    def project_block(j):
        r0 = j * PROJ_ROWS
        hq = hq_ref[r0:r0 + PROJ_ROWS, :]
        q_all = _dot(hq, wq_ref[...])
        for g in range(GROUP):
            cols = slice(g * D_KV, (g + 1) * D_KV)
            q_ref[g, r0:r0 + PROJ_ROWS, :] = _head_rms(q_all[:, cols], seg, qg_ref[...]).astype(BF16)
        zg_ref[r0:r0 + PROJ_ROWS, :] = _silu(_dot(hq, wz_ref[...]))

    def attend_block(j):
        r0 = j * Q_BLOCK
        bsel = jnp.where(t == 0, 0, 1) if j == 0 else 1
        qstack = jnp.concatenate([q_ref[g, r0:r0 + Q_BLOCK, :] for g in range(GROUP)], axis=0)
        out = None
        keys = jnp.concatenate([km_ref[kvh, r0:r0 + span, :] for kvh in range(N_KV)], axis=0)
        s_all = _dot_nt(qstack, keys)
        for kvh in range(N_KV):
            s = s_all[:, kvh * span:(kvh + 1) * span] + bias_ref[bsel, kvh]
            probs = []
            for g in range(GROUP):
                p, denom = _softmax_unnorm(s[g * Q_BLOCK:(g + 1) * Q_BLOCK, :], sink_ref[kvh * GROUP + g])
                probs.append((p * (1.0 / denom)).astype(BF16))
            o = _dot(jnp.concatenate(probs, axis=0), vm_ref[kvh, r0:r0 + span, :])
            out = o if out is None else out + o
        for g in range(GROUP):
            o_ref[r0:r0 + Q_BLOCK, g * D_KV:(g + 1) * D_KV] = out[g * Q_BLOCK:(g + 1) * Q_BLOCK, :]

    def output_block(j):
        rows = slice(j * OUT_ROWS, (j + 1) * OUT_ROWS)
        gated = (o_ref[rows, :] * zg_ref[rows, :]).astype(BF16)
        y_ref[0, rows, :] = x_ref[0, rows, :] + _dot(gated, wout_ref[...])

    per_proj = PROJ_ROWS // Q_BLOCK
    per_out = OUT_ROWS // Q_BLOCK
    project_block(0)
    for j in range(tm // Q_BLOCK):
        if j % per_proj == 0 and (j // per_proj + 1) * PROJ_ROWS < tm:
            project_block(j // per_proj + 1)
        attend_block(j)
        if j > 0 and j % per_out == 0:
            output_block(j // per_out - 1)
    output_block(tm // OUT_ROWS - 1)

    for kvh in range(N_KV):
        mask = _lane_segment_mask(kvh)
        kprev_ref[kvh] = jnp.where(mask, k[tm - WINDOW:tm, :], 0.0).astype(BF16)
        vprev_ref[kvh] = jnp.where(mask, v[tm - WINDOW:tm, :], 0.0).astype(BF16)


def _mixer_b_prompt(x, sinks, gkv, gb, wkv, wq, wz, kg, qg, seg, bias, wout):
    b, t, d = x.shape
    tm = TILE_M
    const = lambda shape: pl.BlockSpec(shape, lambda i, j: (0,) * len(shape), pipeline_mode=pl.Buffered(1))
    return pl.pallas_call(
        _mixer_b_prompt_kernel,
        grid=(b, t // tm),
        in_specs=[
            pl.BlockSpec(memory_space=pltpu.SMEM),
            pl.BlockSpec((1, tm, d), lambda i, j: (i, j, 0)),
            const((1, d)),
            const((1, d)),
            const((d, 2 * D_KV)),
            const((d, D_ATTN)),
            const((d, D_ATTN)),
            const((1, D_KV)),
            const((1, D_KV)),
            const((D_KV, D_KV)),
            const((2, N_KV, GROUP * Q_BLOCK, WINDOW + Q_BLOCK)),
            const((D_ATTN, d)),
        ],
        out_specs=[
            pl.BlockSpec((1, tm, d), lambda i, j: (i, j, 0)),
            pl.BlockSpec((1, WINDOW, D_KV), lambda i, j: (i, 0, 0)),
            pl.BlockSpec((1, WINDOW, D_KV), lambda i, j: (i, 0, 0)),
        ],
        out_shape=[
            jax.ShapeDtypeStruct((b, t, d), F32),
            jax.ShapeDtypeStruct((b, WINDOW, D_KV), F32),
            jax.ShapeDtypeStruct((b, WINDOW, D_KV), F32),
        ],
        scratch_shapes=[
            pltpu.VMEM((N_KV, WINDOW + tm, D_KV), BF16),
            pltpu.VMEM((N_KV, WINDOW + tm, D_KV), BF16),
            pltpu.VMEM((N_KV, WINDOW, D_KV), BF16),
            pltpu.VMEM((N_KV, WINDOW, D_KV), BF16),
            pltpu.VMEM((GROUP, tm, D_KV), BF16),
            pltpu.VMEM((tm, D_ATTN), F32),
            pltpu.VMEM((tm, d), BF16),
            pltpu.VMEM((tm, D_ATTN), F32),
        ],
        compiler_params=pltpu.CompilerParams(
            dimension_semantics=("arbitrary", "arbitrary"), vmem_limit_bytes=VMEM_LIMIT,
            allow_input_fusion=[False, False, False, False, True, True, True, False, False, False, False, True]),
        name="mixer_b_prompt",
    )(sinks, x, gkv, gb, wkv, wq, wz, kg, qg, seg, bias, wout)


def _mixer_b_sample_kernel(x_ref, ck_ref, cv_ref, gkv_ref, gb_ref, wkv_ref, wq_ref, wz_ref, kg_ref,
                           qg_ref, seg_ref, biasc_ref, biasn_ref, sink_ref, wout_ref,
                           y_ref, nk_ref, nv_ref,
                           kn_ref, vn_ref, q_ref, o_ref):
    nb, ts, d = x_ref.shape
    x = x_ref[...].reshape(nb * ts, d)
    k, v, q, zg = _project_kvqz(x, gkv_ref, gb_ref, wkv_ref, wq_ref, wz_ref, kg_ref, qg_ref, seg_ref)
    for g in range(GROUP):
        q_ref[g] = q[g].reshape(nb, ts, D_KV)
    kn_ref[...] = k.reshape(nb, ts, D_KV)
    vn_ref[...] = v.reshape(nb, ts, D_KV)

    masks = [_lane_segment_mask(kvh) for kvh in range(N_KV)]
    bias_c = biasc_ref[...]
    bias_n = biasn_ref[...]
    sink = sink_ref[...]
    new_lanes = lax.broadcasted_iota(jnp.int32, (1, WINDOW), 1) >= WINDOW - ts
    pad_rows = jnp.zeros((NEW_KEY_ROWS - ts, D_KV), F32)
    head_rows = jnp.zeros((WINDOW - ts, D_KV), F32)

    def seq(i, carry):
        k_old = ck_ref[i]
        v_old = cv_ref[i]
        k_new = kn_ref[i]
        v_new = vn_ref[i]
        rows = []
        for g in range(GROUP):
            qg_rows = q_ref[g, i]
            for kvh in range(N_KV):
                rows.append(jnp.where(masks[kvh], qg_rows, jnp.zeros_like(qg_rows)))
        lhs = jnp.concatenate(rows, axis=0).astype(BF16)
        k_new16 = jnp.concatenate([k_new, pad_rows], axis=0).astype(BF16)
        v_new16 = jnp.concatenate([v_new, pad_rows], axis=0).astype(BF16)
        s_c = _dot(lhs, k_old.astype(BF16)) + bias_c
        s_n = _dot_nt(lhs, k_new16) + bias_n
        m = jnp.maximum(jnp.maximum(jnp.max(s_c, axis=-1, keepdims=True), jnp.max(s_n, axis=-1, keepdims=True)),
                        sink)
        p_c = jnp.exp2(s_c - m)
        p_n = jnp.exp2(s_n - m)
        denom = (jnp.sum(p_c, axis=-1, keepdims=True) + jnp.sum(p_n, axis=-1, keepdims=True)
                 + jnp.exp2(sink - m))
        r = 1.0 / denom
        o = _dot_nt((p_c * r).astype(BF16), v_old.astype(BF16)) + _dot((p_n * r).astype(BF16), v_new16)
        for g in range(GROUP):
            acc = None
            for kvh in range(N_KV):
                r0 = (g * N_KV + kvh) * ts
                part = jnp.where(masks[kvh], o[r0:r0 + ts, :], 0.0)
                acc = part if acc is None else acc + part
            o_ref[i, :, g * D_KV:(g + 1) * D_KV] = acc
        k_cols = jnp.concatenate([head_rows, k_new], axis=0).T
        v_cols = jnp.concatenate([head_rows, v_new], axis=0).T
        nk_ref[i] = jnp.where(new_lanes, k_cols, pltpu.roll(k_old, WINDOW - ts, axis=1))
        nv_ref[i] = jnp.where(new_lanes, v_cols, pltpu.roll(v_old, WINDOW - ts, axis=1))
        return carry

    lax.fori_loop(0, nb, seq, 0, unroll=SEQ_UNROLL)

    og = (o_ref[...].reshape(nb * ts, D_ATTN) * zg).astype(BF16)
    y_ref[...] = (x + _dot(og, wout_ref[...])).reshape(nb, ts, d)


def _mixer_b_sample(x, ck_t, cv_t, gkv, gb, wkv, wq, wz, kg, qg, seg, bias_c, bias_n, sink, wout):
    b, ts, d = x.shape
    nb = SAMPLE_BLOCK_ATTN
    const = lambda shape: pl.BlockSpec(shape, lambda i: (0,) * len(shape))
    rows = N_HEADS * ts
    return pl.pallas_call(
        _mixer_b_sample_kernel,
        grid=(b // nb,),
        in_specs=[
            pl.BlockSpec((nb, ts, d), lambda i: (i, 0, 0)),
            pl.BlockSpec((nb, D_KV, WINDOW), lambda i: (i, 0, 0)),
            pl.BlockSpec((nb, D_KV, WINDOW), lambda i: (i, 0, 0)),
            const((1, d)),
            const((1, d)),
            const((d, 2 * D_KV)),
            const((d, D_ATTN)),
            const((d, D_ATTN)),
            const((1, D_KV)),
            const((1, D_KV)),
            const((D_KV, D_KV)),
            const((rows, WINDOW)),
            const((rows, NEW_KEY_ROWS)),
            const((rows, 1)),
            const((D_ATTN, d)),
        ],
        out_specs=[
            pl.BlockSpec((nb, ts, d), lambda i: (i, 0, 0)),
            pl.BlockSpec((nb, D_KV, WINDOW), lambda i: (i, 0, 0)),
            pl.BlockSpec((nb, D_KV, WINDOW), lambda i: (i, 0, 0)),
        ],
        out_shape=[
            jax.ShapeDtypeStruct((b, ts, d), F32),
            jax.ShapeDtypeStruct((b, D_KV, WINDOW), F32),
            jax.ShapeDtypeStruct((b, D_KV, WINDOW), F32),
        ],
        scratch_shapes=[
            pltpu.VMEM((nb, ts, D_KV), F32),
            pltpu.VMEM((nb, ts, D_KV), F32),
            pltpu.VMEM((GROUP, nb, ts, D_KV), F32),
            pltpu.VMEM((nb, ts, D_ATTN), F32),
        ],
        compiler_params=pltpu.CompilerParams(
            dimension_semantics=("arbitrary",), vmem_limit_bytes=VMEM_LIMIT),
        name="mixer_b_sample",
    )(x, ck_t, cv_t, gkv, gb, wkv, wq, wz, kg, qg, seg, bias_c, bias_n, sink, wout)


def _alibi_slopes():
    return 2.0 ** (-8.0 * np.arange(1, N_HEADS + 1) / N_HEADS)


def _band_bias(n_query, n_key, first_block):
    dist = np.arange(n_query)[:, None] + WINDOW - np.arange(n_key)[None, :]
    allowed = (dist >= 0) & (dist < WINDOW)
    if first_block:
        allowed = allowed & (np.arange(n_key)[None, :] >= WINDOW)
    bias = -_alibi_slopes()[:, None, None] * dist[None].astype(np.float64) * LOG2E
    return np.where(allowed[None], bias, -np.inf).astype(np.float32)


def _group_major(w, axis):
    shape = w.shape
    w = w.reshape(shape[:axis] + (N_KV, GROUP, HEAD_DIM) + shape[axis + 1:])
    w = jnp.swapaxes(w, axis, axis + 1)
    return w.reshape(shape)


def kernel(x_prompt, x_sample, state_conv, cache_k, cache_v, norm_a, w_in_a, w_dw_a, b_dw_a, ln_g_a, ln_b_a, w_out_a, norm_kv, w_kv, k_norm, norm_b, w_in_b, q_norm, sinks_b, w_out_b):
    assert norm_a.shape[0] == 1 and norm_b.shape[0] == 1
    row = lambda v: v.reshape(1, -1).astype(F32)

    a_args = (row(norm_a[0]), w_in_a[0].astype(BF16), w_dw_a[0].astype(F32), row(b_dw_a[0]),
              row(ln_g_a[0]), row(ln_b_a[0]), w_out_a[0].astype(BF16))
    xp, st_p = _mixer_a_prompt(x_prompt, *a_args)
    xs, st_s = _mixer_a_sample(x_sample, jnp.transpose(state_conv[0], (1, 0, 2)), *a_args)
    st_s = jnp.transpose(st_s, (1, 0, 2))[None]

    wq = _group_major(w_in_b[0][:, :D_ATTN], 1).astype(BF16)
    wz = _group_major(w_in_b[0][:, D_ATTN:], 1).astype(BF16)
    wout = _group_major(w_out_b[0], 0).astype(BF16)
    sinks = sinks_b[0].astype(F32) * LOG2E
    kg = row(jnp.tile(k_norm, N_KV))
    qg = row(jnp.tile(q_norm[0], N_KV)) * (HEAD_DIM ** -0.5 * LOG2E)
    seg = jnp.asarray(np.kron(np.eye(N_KV), np.full((HEAD_DIM, HEAD_DIM), 1.0 / HEAD_DIM)), BF16)
    shared = (row(norm_kv), row(norm_b[0]), w_kv.astype(BF16), wq, wz, kg, qg, seg)

    span = WINDOW + Q_BLOCK
    bias_p = jnp.asarray(np.stack([_band_bias(Q_BLOCK, span, True), _band_bias(Q_BLOCK, span, False)])
                         .reshape(2, N_KV, GROUP * Q_BLOCK, span))
    yp, k_tail, v_tail = _mixer_b_prompt(xp, sinks, *shared, bias_p, wout)

    ts = x_sample.shape[1]
    head_of_row = np.array([kvh * GROUP + g for g in range(GROUP) for kvh in range(N_KV)])
    bias_s = _band_bias(ts, WINDOW + NEW_KEY_ROWS, False)[head_of_row].reshape(N_HEADS * ts, WINDOW + NEW_KEY_ROWS)
    sink_s = jnp.repeat(sinks[head_of_row], ts).reshape(N_HEADS * ts, 1)
    nb = cache_k.shape[0]
    to_cols = lambda c: jnp.transpose(c, (0, 2, 3, 1)).reshape(nb, D_KV, WINDOW)
    from_cols = lambda c: jnp.transpose(c.reshape(nb, N_KV, HEAD_DIM, WINDOW), (0, 3, 1, 2))
    ys, nk, nv = _mixer_b_sample(xs, to_cols(cache_k), to_cols(cache_v), *shared,
                                 jnp.asarray(bias_s[:, :WINDOW]), jnp.asarray(bias_s[:, WINDOW:]), sink_s, wout)

    kv4 = lambda a: a.reshape(a.shape[0], WINDOW, N_KV, HEAD_DIM)
    return (yp, ys, st_p[None, :, PREFIX_SKIP:, :], st_s, kv4(k_tail), kv4(v_tail), from_cols(nk), from_cols(nv))
```
